```python
import math
import jax, jax.numpy as jnp
from jax import lax
import numpy as np

D_MODEL = 1024
BATCH = 8
SEQ = 4096
DEPTH = 4

N_A = DEPTH // 2
N_B = DEPTH - N_A
POOL_WINDOWS = (2, 4, 8, 16)
POOL_GROUPS = len(POOL_WINDOWS)
GROUP_W = D_MODEL // POOL_GROUPS
MLA_HEADS = 8
QK_NOPE = 128
QK_ROPE = 64
V_HEAD = 128
Q_RANK = 768
KV_RANK = 256
ROPE_THETA = 10000.0
ATTN_SCALE = 1.0 / math.sqrt(QK_NOPE + QK_ROPE)
Q_BLOCK = 128
PEER_HEADS = 8
N_KEYS = 128
N_EXPERTS = N_KEYS * N_KEYS
PEER_TOPK = 16
PEER_DQ = 256
PEER_DHALF = PEER_DQ // 2
TOKEN_CHUNK = 128
PLE_DIM = 256
EPS = 1e-6

kernel_name = 'yoco_pool_mla_peer_ple_trunk'


def rmsnorm(x, g):
    x32 = x.astype(jnp.float32)
    y = x32 * lax.rsqrt(jnp.mean(x32 * x32, axis=-1, keepdims=True) + EPS)
    return (y * g.astype(jnp.float32)).astype(x.dtype)


def rope_tables(positions, dtype):
    inv = ROPE_THETA ** (-jnp.arange(0, QK_ROPE, 2, dtype=jnp.float32) / QK_ROPE)
    ang = positions.astype(jnp.float32)[..., None] * inv
    return jnp.cos(ang).astype(dtype), jnp.sin(ang).astype(dtype)


def apply_rope(x, cos, sin):
    half = x.shape[-1] // 2
    x1, x2 = x[..., :half], x[..., half:]
    return jnp.concatenate([x1 * cos - x2 * sin, x2 * cos + x1 * sin], axis=-1)


def pool_mixer(h, w_groups, scale):
    B, S, D = h.shape
    h32 = h.astype(jnp.float32)
    cs = jnp.concatenate([jnp.zeros((B, 1, D), jnp.float32), jnp.cumsum(h32, axis=1)], axis=1)
    t = jnp.arange(S)
    outs = []
    for g, w in enumerate(POOL_WINDOWS):
        sl = slice(g * GROUP_W, (g + 1) * GROUP_W)
        win_sum = cs[:, 1:, sl] - cs[:, jnp.maximum(t + 1 - w, 0), sl]
        count = jnp.minimum(t + 1, w).astype(jnp.float32)[None, :, None]
        pooled = (win_sum / count - h32[..., sl]).astype(h.dtype)
        outs.append(pooled @ w_groups[g])
    return jnp.concatenate(outs, axis=-1) * scale


def shared_kv(x, kv_in_norm, w_dkv, kv_norm, w_ukv, cos, sin):
    B, S, _ = x.shape
    h = rmsnorm(x, kv_in_norm)
    ckv_kpe = h @ w_dkv
    c_kv = rmsnorm(ckv_kpe[..., :KV_RANK], kv_norm)
    k_pe = apply_rope(ckv_kpe[..., KV_RANK:], cos, sin)
    kv = (c_kv @ w_ukv).reshape(B, S, MLA_HEADS, QK_NOPE + V_HEAD)
    return kv[..., :QK_NOPE], k_pe, kv[..., QK_NOPE:]


def causal_attention(q_nope, q_pe, k_nope, k_pe, v):
    B, S, H, _ = q_nope.shape
    nb = S // Q_BLOCK
    qn = q_nope.reshape(B, nb, Q_BLOCK, H, QK_NOPE).transpose(1, 0, 2, 3, 4)
    qp = q_pe.reshape(B, nb, Q_BLOCK, H, QK_ROPE).transpose(1, 0, 2, 3, 4)
    kpos = jnp.arange(S)

    def block(args):
        qn_b, qp_b, bi = args
        s = (jnp.einsum('bqhd,bkhd->bhqk', qn_b, k_nope).astype(jnp.float32)
             + jnp.einsum('bqhr,bkr->bhqk', qp_b, k_pe).astype(jnp.float32)) * ATTN_SCALE
        qpos = bi * Q_BLOCK + jnp.arange(Q_BLOCK)
        s = jnp.where(kpos[None, :] <= qpos[:, None], s, -jnp.inf)
        pr = jax.nn.softmax(s, axis=-1).astype(v.dtype)
        return jnp.einsum('bhqk,bkhd->bqhd', pr, v)

    o = lax.map(block, (qn, qp, jnp.arange(nb)))
    return o.transpose(1, 0, 2, 3, 4).reshape(B, S, H * V_HEAD)


def mla_mixer(h, w_dq, q_norm, w_uq, w_o, k_nope, k_pe, v, cos, sin):
    B, S, _ = h.shape
    c_q = rmsnorm(h @ w_dq, q_norm)
    q = (c_q @ w_uq).reshape(B, S, MLA_HEADS, QK_NOPE + QK_ROPE)
    q_nope = q[..., :QK_NOPE]
    q_pe = apply_rope(q[..., QK_NOPE:], cos[:, :, None, :], sin[:, :, None, :])
    return causal_attention(q_nope, q_pe, k_nope, k_pe, v) @ w_o


def peer_mixer(h, w_q, sub_keys, u, v):
    B, S, D = h.shape
    q = (h @ w_q).astype(jnp.float32).reshape(B, S, PEER_HEADS, 2, PEER_DHALF)
    s = jnp.einsum('bshpc,hpnc->bshpn', q, sub_keys.astype(jnp.float32))
    top_s, top_i = lax.top_k(s, PEER_TOPK)
    cand_s = (top_s[..., 0, :, None] + top_s[..., 1, None, :]).reshape(B, S, PEER_HEADS, PEER_TOPK * PEER_TOPK)
    cand_i = (top_i[..., 0, :, None] * N_KEYS + top_i[..., 1, None, :]).reshape(B, S, PEER_HEADS, PEER_TOPK * PEER_TOPK)
    best_s, pos = lax.top_k(cand_s, PEER_TOPK)
    idx = jnp.take_along_axis(cand_i, pos, axis=-1)
    gate = jax.nn.softmax(best_s, axis=-1)
    n_chunks = (B * S) // TOKEN_CHUNK
    hc = h.reshape(n_chunks, TOKEN_CHUNK, D)
    ic = idx.reshape(n_chunks, TOKEN_CHUNK, PEER_HEADS * PEER_TOPK)
    gc = gate.reshape(n_chunks, TOKEN_CHUNK, PEER_HEADS * PEER_TOPK).astype(h.dtype)

    def chunk(args):
        hh, ii, gg = args
        a = jnp.einsum('tkd,td->tk', u[ii], hh)
        w = gg * jax.nn.gelu(a)
        return jnp.einsum('tk,tkd->td', w, v[ii])

    return lax.map(chunk, (hc, ic, gc)).reshape(B, S, D)


def setup_inputs(seed: int = 0) -> dict:
    key = jax.random.key(seed)
    ks = jax.random.split(key, 24)
    f32 = jnp.float32

    def nrm(k, shape, fan_in):
        return jax.random.normal(k, shape, f32) * (fan_in ** -0.5)

    def gain(k, shape):
        return 1.0 + 0.02 * jax.random.normal(k, shape, f32)

    D = D_MODEL
    return {
        'x': jax.random.normal(ks[0], (BATCH, SEQ, D), f32),
        'p': jax.random.normal(ks[1], (DEPTH, BATCH, SEQ, PLE_DIM), f32),
        'positions': jax.random.randint(ks[2], (BATCH, 1), 0, 1024, jnp.int32) + jnp.arange(SEQ, dtype=jnp.int32)[None, :],
        'mix_norm': gain(ks[3], (DEPTH, D)),
        'ffn_norm': gain(ks[4], (DEPTH, D)),
        'pool_w': nrm(ks[5], (N_A, POOL_GROUPS, GROUP_W, GROUP_W), GROUP_W),
        'pool_scale': gain(ks[6], (N_A, D)),
        'kv_in_norm': gain(ks[7], (D,)),
        'w_dkv': nrm(ks[8], (D, KV_RANK + QK_ROPE), D),
        'kv_norm': gain(ks[9], (KV_RANK,)),
        'w_ukv': nrm(ks[10], (KV_RANK, MLA_HEADS * (QK_NOPE + V_HEAD)), KV_RANK),
        'w_dq': nrm(ks[11], (N_B, D, Q_RANK), D),
        'q_norm': gain(ks[12], (N_B, Q_RANK)),
        'w_uq': nrm(ks[13], (N_B, Q_RANK, MLA_HEADS * (QK_NOPE + QK_ROPE)), Q_RANK),
        'w_o': nrm(ks[14], (N_B, MLA_HEADS * V_HEAD, D), MLA_HEADS * V_HEAD),
        'peer_wq': nrm(ks[15], (DEPTH, D, PEER_HEADS * PEER_DQ), D),
        'peer_keys': nrm(ks[16], (DEPTH, PEER_HEADS, 2, N_KEYS, PEER_DHALF), PEER_DHALF),
        'peer_u': nrm(ks[17], (DEPTH, N_EXPERTS, D), D),
        'peer_v': nrm(ks[18], (DEPTH, N_EXPERTS, D), PEER_HEADS),
        'ple_norm': gain(ks[19], (DEPTH, D)),
        'ple_wg': nrm(ks[20], (DEPTH, D, D), D),
        'ple_wp': nrm(ks[21], (DEPTH, PLE_DIM, D), PLE_DIM),
        'final_norm': gain(ks[22], (D,)),
    }


def reference(x, p, positions, mix_norm, ffn_norm, pool_w, pool_scale, kv_in_norm, w_dkv, kv_norm, w_ukv,
              w_dq, q_norm, w_uq, w_o, peer_wq, peer_keys, peer_u, peer_v, ple_norm, ple_wg, ple_wp, final_norm):
    cos, sin = rope_tables(positions, x.dtype)
    k_nope = k_pe = v_sh = None
    for i in range(DEPTH):
        h = rmsnorm(x, mix_norm[i])
        if i < N_A:
            x = x + pool_mixer(h, pool_w[i], pool_scale[i])
        else:
            if i == N_A:
                k_nope, k_pe, v_sh = shared_kv(x, kv_in_norm, w_dkv, kv_norm, w_ukv, cos, sin)
            j = i - N_A
            x = x + mla_mixer(h, w_dq[j], q_norm[j], w_uq[j], w_o[j], k_nope, k_pe, v_sh, cos, sin)
        x = x + peer_mixer(rmsnorm(x, ffn_norm[i]), peer_wq[i], peer_keys[i], peer_u[i], peer_v[i])
        g = jax.nn.sigmoid((rmsnorm(x, ple_norm[i]) @ ple_wg[i]).astype(jnp.float32)).astype(x.dtype)
        x = x + g * (p[i] @ ple_wp[i])
    return rmsnorm(x, final_norm)
```

```python
import functools
import math

import jax
import jax.numpy as jnp
from jax import lax
from jax.experimental import pallas as pl
from jax.experimental.pallas import tpu as pltpu

F32 = jnp.float32
HI = lax.Precision.HIGHEST
EPS = 1e-6

POOL_WINDOWS = (2, 4, 8, 16)
POOL_HALO = 16
MLA_HEADS = 8
QK_NOPE = 128
QK_ROPE = 64
V_HEAD = 128
KV_RANK = 256
ROPE_THETA = 10000.0
ATTN_SCALE = 1.0 / math.sqrt(QK_NOPE + QK_ROPE)
PEER_HEADS = 8
N_KEYS = 128
PEER_TOPK = 16
PEER_DHALF = 128

ROW_BLOCK = 256
POOL_BLOCK = 512
ATTN_BLOCK = 512
GATHER_SUB = 8
ISSUE_UNROLL = 8
VMEM_LIMIT = 56 * 1024 * 1024


def _params(sem):
    return pltpu.CompilerParams(dimension_semantics=sem, vmem_limit_bytes=VMEM_LIMIT)


def _rms(x, g):
    ms = jnp.mean(x * x, axis=-1, keepdims=True)
    return x * lax.rsqrt(ms + EPS) * g


def _dot(a, b):
    return jnp.dot(a, b, precision=HI, preferred_element_type=F32)


def _dot_nt(a, b):
    return lax.dot_general(a, b, (((1,), (1,)), ((), ())), precision=HI, preferred_element_type=F32)


def _full(shape):
    return pl.BlockSpec(shape, lambda *_: (0,) * len(shape))


def _pool_kernel(x_ref, halo_ref, g_ref, w_ref, sc_ref, o_ref, *, ts, blocks_per_seq):
    blk = pl.program_id(0) % blocks_per_seq
    x = x_ref[...]
    g = g_ref[...]
    h = _rms(x, g)
    hh = jnp.where(blk == 0, 0.0, _rms(halo_ref[...], g))
    full = jnp.concatenate([hh, h], axis=0)
    t = blk * ts + lax.broadcasted_iota(jnp.int32, (ts, 1), 0)
    gw = w_ref.shape[-1]
    outs = []
    for gi, w in enumerate(POOL_WINDOWS):
        p = full[:, gi * gw:(gi + 1) * gw]
        step = 1
        while step < w:
            p = p + pltpu.roll(p, step, axis=0)
            step *= 2
        win = p[POOL_HALO:, :]
        cnt = jnp.minimum(t + 1, w).astype(F32)
        pooled = win / cnt - h[:, gi * gw:(gi + 1) * gw]
        outs.append(_dot(pooled, w_ref[gi]))
    o_ref[...] = x + jnp.concatenate(outs, axis=-1) * sc_ref[...]


def _pool_layer(x2, seq, g, w, scale):
    n, d = x2.shape
    ts = min(POOL_BLOCK, seq)
    assert seq % ts == 0 and ts % POOL_HALO == 0 and all(wd & (wd - 1) == 0 and wd <= POOL_HALO for wd in POOL_WINDOWS)
    hb = ts // POOL_HALO
    return pl.pallas_call(
        functools.partial(_pool_kernel, ts=ts, blocks_per_seq=seq // ts),
        grid=(n // ts,),
        in_specs=[
            pl.BlockSpec((ts, d), lambda i: (i, 0)),
            pl.BlockSpec((POOL_HALO, d), lambda i: (jnp.maximum(i * hb - 1, 0), 0)),
            _full((1, d)),
            _full(w.shape),
            _full((1, d)),
        ],
        out_specs=pl.BlockSpec((ts, d), lambda i: (i, 0)),
        out_shape=jax.ShapeDtypeStruct((n, d), F32),
        compiler_params=_params(("parallel",)),
        name="pool_mixer",
    )(x2, x2, g.reshape(1, d), w, scale.reshape(1, d))


def _rope_kernel(pos_ref, inv_ref, cos_ref, sin_ref):
    ang = inv_ref[...] * pos_ref[0].astype(F32)
    cos_ref[0] = jnp.cos(ang)
    sin_ref[0] = jnp.sin(ang)


def _rope_tables(positions):
    b, s = positions.shape
    half = QK_ROPE // 2
    inv = ROPE_THETA ** (-jnp.arange(0, QK_ROPE, 2, dtype=F32) / QK_ROPE)
    cos_t, sin_t = pl.pallas_call(
        _rope_kernel,
        grid=(b,),
        in_specs=[pl.BlockSpec((1, 1, s), lambda i: (i, 0, 0)), _full((half, 1))],
        out_specs=[pl.BlockSpec((1, half, s), lambda i: (i, 0, 0))] * 2,
        out_shape=[jax.ShapeDtypeStruct((b, half, s), F32)] * 2,
        compiler_params=_params(("parallel",)),
        name="rope_tables",
    )(positions.reshape(b, 1, s), inv.reshape(half, 1))
    to_rows = lambda a: a.transpose(0, 2, 1).reshape(b * s, half)
    return to_rows(cos_t), to_rows(sin_t)


def _kv_kernel(x_ref, cos_ref, sin_ref, gin_ref, wdkv_ref, gkv_ref, wukv_ref, k_ref, v_ref):
    h = _rms(x_ref[...], gin_ref[...])
    ckv = _dot(h, wdkv_ref[...])
    c = _rms(ckv[:, :KV_RANK], gkv_ref[...])
    half = QK_ROPE // 2
    x1 = ckv[:, KV_RANK:KV_RANK + half]
    x2 = ckv[:, KV_RANK + half:]
    cos = cos_ref[...]
    sin = sin_ref[...]
    k1 = x1 * cos - x2 * sin
    k2 = x2 * cos + x1 * sin
    kv = _dot(c, wukv_ref[...])
    for hd in range(MLA_HEADS):
        k_ref[hd] = jnp.concatenate([kv[:, hd * QK_NOPE:(hd + 1) * QK_NOPE], k1, k2], axis=-1)
        off = MLA_HEADS * QK_NOPE + hd * V_HEAD
        v_ref[hd] = kv[:, off:off + V_HEAD]


def _split_heads(w, bounds):
    w3 = w.reshape(w.shape[0], MLA_HEADS, -1)
    return jnp.concatenate([w3[:, :, lo:hi].reshape(w.shape[0], -1) for lo, hi in bounds], axis=1)


def _shared_kv(x2, cos, sin, kv_in_norm, w_dkv, kv_norm, w_ukv):
    n, d = x2.shape
    t = ROW_BLOCK
    half = QK_ROPE // 2
    w_ukv_p = _split_heads(w_ukv, [(0, QK_NOPE), (QK_NOPE, QK_NOPE + V_HEAD)])
    row = lambda width: pl.BlockSpec((t, width), lambda i: (i, 0))
    return pl.pallas_call(
        _kv_kernel,
        grid=(n // t,),
        in_specs=[row(d), row(half), row(half), _full((1, d)), _full(w_dkv.shape), _full((1, KV_RANK)),
                  _full(w_ukv_p.shape)],
        out_specs=[pl.BlockSpec((MLA_HEADS, t, QK_NOPE + QK_ROPE), lambda i: (0, i, 0)),
                   pl.BlockSpec((MLA_HEADS, t, V_HEAD), lambda i: (0, i, 0))],
        out_shape=[jax.ShapeDtypeStruct((MLA_HEADS, n, QK_NOPE + QK_ROPE), F32),
                   jax.ShapeDtypeStruct((MLA_HEADS, n, V_HEAD), F32)],
        compiler_params=_params(("parallel",)),
        name="shared_kv",
    )(x2, cos, sin, kv_in_norm.reshape(1, d), w_dkv, kv_norm.reshape(1, KV_RANK), w_ukv_p)


def _q_kernel(x_ref, cos_ref, sin_ref, g_ref, wdq_ref, gq_ref, wuq_ref, q_ref):
    h = _rms(x_ref[...], g_ref[...])
    cq = _rms(_dot(h, wdq_ref[...]), gq_ref[...])
    q = _dot(cq, wuq_ref[...]) * ATTN_SCALE
    half = QK_ROPE // 2
    nope = MLA_HEADS * QK_NOPE
    cos = jnp.concatenate([cos_ref[...]] * MLA_HEADS, axis=-1)
    sin = jnp.concatenate([sin_ref[...]] * MLA_HEADS, axis=-1)
    p1 = q[:, nope:nope + MLA_HEADS * half]
    p2 = q[:, nope + MLA_HEADS * half:]
    r1 = p1 * cos - p2 * sin
    r2 = p2 * cos + p1 * sin
    for hd in range(MLA_HEADS):
        q_ref[hd] = jnp.concatenate([q[:, hd * QK_NOPE:(hd + 1) * QK_NOPE], r1[:, hd * half:(hd + 1) * half],
                                     r2[:, hd * half:(hd + 1) * half]], axis=-1)


def _q_proj(x2, cos, sin, g, w_dq, q_norm, w_uq):
    n, d = x2.shape
    t = ROW_BLOCK
    half = QK_ROPE // 2
    rank = w_dq.shape[1]
    w_uq_p = _split_heads(w_uq, [(0, QK_NOPE), (QK_NOPE, QK_NOPE + half), (QK_NOPE + half, QK_NOPE + QK_ROPE)])
    row = lambda width: pl.BlockSpec((t, width), lambda i: (i, 0))
    return pl.pallas_call(
        _q_kernel,
        grid=(n // t,),
        in_specs=[row(d), row(half), row(half), _full((1, d)), _full(w_dq.shape), _full((1, rank)),
                  _full(w_uq_p.shape)],
        out_specs=pl.BlockSpec((MLA_HEADS, t, QK_NOPE + QK_ROPE), lambda i: (0, i, 0)),
        out_shape=jax.ShapeDtypeStruct((MLA_HEADS, n, QK_NOPE + QK_ROPE), F32),
        compiler_params=_params(("parallel",)),
        name="mla_q",
    )(x2, cos, sin, g.reshape(1, d), w_dq, q_norm.reshape(1, rank), w_uq_p)


def _flash_kernel(q_ref, k_ref, v_ref, o_ref, m_sc, l_sc, acc_sc, *, tb):
    qi = pl.program_id(2)
    ki = pl.program_id(3)

    @pl.when(ki == 0)
    def _():
        m_sc[...] = jnp.full_like(m_sc, -jnp.inf)
        l_sc[...] = jnp.zeros_like(l_sc)
        acc_sc[...] = jnp.zeros_like(acc_sc)

    @pl.when(ki <= qi)
    def _():
        s = _dot_nt(q_ref[0], k_ref[0])
        qpos = qi * tb + lax.broadcasted_iota(jnp.int32, (tb, tb), 0)
        kpos = ki * tb + lax.broadcasted_iota(jnp.int32, (tb, tb), 1)
        s = jnp.where(kpos <= qpos, s, -jnp.inf)
        m_prev = m_sc[...]
        m_new = jnp.maximum(m_prev, jnp.max(s, axis=-1, keepdims=True))
        p = jnp.exp(s - m_new)
        alpha = jnp.exp(m_prev - m_new)
        l_sc[...] = alpha * l_sc[...] + jnp.sum(p, axis=-1, keepdims=True)
        acc_sc[...] = alpha * acc_sc[...] + _dot(p, v_ref[0])
        m_sc[...] = m_new

    @pl.when(ki == qi)
    def _():
        o_ref[...] = acc_sc[...] / l_sc[...]


def _flash_attention(q, k, v, batch, seq):
    tb = min(ATTN_BLOCK, seq)
    nb = seq // tb
    n = batch * seq
    dk = q.shape[-1]
    return pl.pallas_call(
        functools.partial(_flash_kernel, tb=tb),
        grid=(batch, MLA_HEADS, nb, nb),
        in_specs=[
            pl.BlockSpec((1, tb, dk), lambda b, h, qi, ki: (h, b * nb + qi, 0)),
            pl.BlockSpec((1, tb, dk), lambda b, h, qi, ki: (h, b * nb + jnp.minimum(ki, qi), 0)),
            pl.BlockSpec((1, tb, V_HEAD), lambda b, h, qi, ki: (h, b * nb + jnp.minimum(ki, qi), 0)),
        ],
        out_specs=pl.BlockSpec((tb, V_HEAD), lambda b, h, qi, ki: (b * nb + qi, h)),
        out_shape=jax.ShapeDtypeStruct((n, MLA_HEADS * V_HEAD), F32),
        scratch_shapes=[pltpu.VMEM((tb, 1), F32), pltpu.VMEM((tb, 1), F32), pltpu.VMEM((tb, V_HEAD), F32)],
        compiler_params=_params(("parallel", "parallel", "parallel", "arbitrary")),
        name="mla_flash",
    )(q, k, v)


def _oproj_kernel(x_ref, o_ref, w_ref, y_ref):
    y_ref[...] = x_ref[...] + _dot(o_ref[...], w_ref[...])


def _out_proj(x2, o, w_o):
    n, d = x2.shape
    t = ROW_BLOCK
    row = lambda width: pl.BlockSpec((t, width), lambda i: (i, 0))
    return pl.pallas_call(
        _oproj_kernel,
        grid=(n // t,),
        in_specs=[row(d), row(o.shape[1]), _full(w_o.shape)],
        out_specs=row(d),
        out_shape=jax.ShapeDtypeStruct((n, d), F32),
        compiler_params=_params(("parallel",)),
        name="mla_out",
    )(x2, o, w_o)


def _top_rows(s, k, payload=None):
    rows, t = s.shape
    iota = lax.broadcasted_iota(jnp.int32, (rows, t), 0)
    out_row = lax.broadcasted_iota(jnp.int32, (k, t), 0)

    def body(r, carry):
        s, vals, idxs = carry
        m = jnp.max(s, axis=0, keepdims=True)
        pos = jnp.min(jnp.where(s == m, iota, rows), axis=0, keepdims=True)
        hit = iota == pos
        pick = pos if payload is None else jnp.sum(jnp.where(hit, payload, 0), axis=0, keepdims=True)
        vals = jnp.where(out_row == r, m, vals)
        idxs = jnp.where(out_row == r, pick, idxs)
        return jnp.where(hit, -jnp.inf, s), vals, idxs

    _, vals, idxs = lax.fori_loop(0, k, body, (s, jnp.zeros((k, t), F32), jnp.zeros((k, t), jnp.int32)))
    return vals, idxs


def _peer_front_kernel(x_ref, g_ref, wq_ref, keys_ref, hn_ref, idx_ref, gate_ref, *, lane_chunk):
    hn = _rms(x_ref[...], g_ref[...])

    @pl.when(pl.program_id(1) == 0)
    def _():
        hn_ref[...] = hn

    q = _dot(hn, wq_ref[...])
    t = hn.shape[0]
    for c in range(t // lane_chunk):
        tok = slice(c * lane_chunk, (c + 1) * lane_chunk)
        tops = []
        for half in range(2):
            s_t = _dot_nt(keys_ref[0, half], q[tok, half * PEER_DHALF:(half + 1) * PEER_DHALF])
            tops.append(_top_rows(s_t, PEER_TOPK))
        (s1, i1), (s2, i2) = tops
        cand_s = jnp.concatenate([s1[a:a + 1, :] + s2 for a in range(PEER_TOPK)], axis=0)
        cand_i = jnp.concatenate([i1[a:a + 1, :] * N_KEYS + i2 for a in range(PEER_TOPK)], axis=0)
        best_s, best_i = _top_rows(cand_s, PEER_TOPK, payload=cand_i)
        e = jnp.exp(best_s - best_s[0:1, :])
        gate_ref[0, :, tok] = e / jnp.sum(e, axis=0, keepdims=True)
        idx_ref[0, :, tok] = best_i


def _peer_front(x2, g, w_q, keys):
    n, d = x2.shape
    t = ROW_BLOCK
    nb = n // t
    dq = 2 * PEER_DHALF
    k = PEER_HEADS * PEER_TOPK
    return pl.pallas_call(
        functools.partial(_peer_front_kernel, lane_chunk=128),
        grid=(nb, PEER_HEADS),
        in_specs=[
            pl.BlockSpec((t, d), lambda i, h: (i, 0)),
            _full((1, d)),
            pl.BlockSpec((d, dq), lambda i, h: (0, h)),
            pl.BlockSpec((1, 2, N_KEYS, PEER_DHALF), lambda i, h: (h, 0, 0, 0)),
        ],
        out_specs=[
            pl.BlockSpec((t, d), lambda i, h: (i, 0)),
            pl.BlockSpec((1, PEER_TOPK, t), lambda i, h: (i, h, 0)),
            pl.BlockSpec((1, PEER_TOPK, t), lambda i, h: (i, h, 0)),
        ],
        out_shape=[
            jax.ShapeDtypeStruct((n, d), F32),
            jax.ShapeDtypeStruct((nb, k, t), jnp.int32),
            jax.ShapeDtypeStruct((nb, k, t), F32),
        ],
        compiler_params=_params(("parallel", "arbitrary")),
        name="peer_front",
    )(x2, g.reshape(1, d), w_q, keys)


def _peer_experts_kernel(idx_ref, hn_ref, gate_ref, x_ref, u_hbm, v_hbm, o_ref, ubuf, vbuf, sem, *, sub):
    t, d = hn_ref.shape
    k = gate_ref.shape[1]
    rows = sub * k
    n_sub = t // sub

    def row_copy(table, buf, which, slot, e, j):
        return pltpu.make_async_copy(table.at[pl.ds(e, 1)], buf.at[slot, pl.ds(j, 1)], sem.at[which, slot])

    def issue(sb, slot):
        for tk in range(sub):
            def body(g, carry, tk=tk):
                for r in range(ISSUE_UNROLL):
                    kk = g * ISSUE_UNROLL + r
                    e = idx_ref[sb * sub + tk, kk]
                    row_copy(u_hbm, ubuf, 0, slot, e, tk * k + kk).start()
                    row_copy(v_hbm, vbuf, 1, slot, e, tk * k + kk).start(priority=1)
                return carry

            lax.fori_loop(0, k // ISSUE_UNROLL, body, 0)

    def wait(slot):
        pltpu.make_async_copy(u_hbm.at[pl.ds(0, rows)], ubuf.at[slot], sem.at[0, slot]).wait()
        pltpu.make_async_copy(v_hbm.at[pl.ds(0, rows)], vbuf.at[slot], sem.at[1, slot]).wait()

    lane = lax.broadcasted_iota(jnp.int32, (1, t), 1)

    def compute(sb, slot):
        for j in range(sub):
            tok = sb * sub + j
            xrow = hn_ref[pl.ds(tok, 1), :]
            a = jnp.sum(ubuf[slot, j * k:(j + 1) * k, :] * xrow, axis=-1, keepdims=True)
            gcol = jnp.sum(jnp.where(lane == tok, gate_ref[0], 0.0), axis=-1, keepdims=True)
            w = gcol * jax.nn.gelu(a)
            out = jnp.sum(w * vbuf[slot, j * k:(j + 1) * k, :], axis=0, keepdims=True)
            o_ref[pl.ds(tok, 1), :] = x_ref[pl.ds(tok, 1), :] + out

    issue(0, 0)

    def step(sb, carry):
        slot = sb % 2

        @pl.when(sb + 1 < n_sub)
        def _():
            issue(sb + 1, 1 - slot)

        wait(slot)
        compute(sb, slot)
        return carry

    lax.fori_loop(0, n_sub, step, 0)


def _peer_experts(x2, hn, idx_rows, gate_t, u, v):
    n, d = x2.shape
    nb, k, t = gate_t.shape
    sub = GATHER_SUB
    assert t % sub == 0
    row = pl.BlockSpec((t, d), lambda i: (i, 0))
    return pl.pallas_call(
        functools.partial(_peer_experts_kernel, sub=sub),
        grid=(nb,),
        in_specs=[
            pl.BlockSpec((t, k), lambda i: (i, 0), memory_space=pltpu.SMEM),
            row,
            pl.BlockSpec((1, k, t), lambda i: (i, 0, 0)),
            row,
            pl.BlockSpec(memory_space=pl.ANY),
            pl.BlockSpec(memory_space=pl.ANY),
        ],
        out_specs=row,
        out_shape=jax.ShapeDtypeStruct((n, d), F32),
        scratch_shapes=[
            pltpu.VMEM((2, sub * k, d), F32),
            pltpu.VMEM((2, sub * k, d), F32),
            pltpu.SemaphoreType.DMA((2, 2)),
        ],
        compiler_params=_params(("arbitrary",)),
        name="peer_experts",
    )(idx_rows, hn, gate_t, x2, u, v)


def _peer_layer(x2, g, w_q, keys, u, v):
    hn, idx_t, gate_t = _peer_front(x2, g, w_q, keys)
    nb, k, t = idx_t.shape
    idx_rows = idx_t.transpose(0, 2, 1).reshape(nb * t, k)
    return _peer_experts(x2, hn, idx_rows, gate_t, u, v)


def _ple_kernel(x_ref, p_ref, g_ref, wg_ref, wp_ref, fg_ref, o_ref, *, final):
    x = x_ref[...]
    gate = jax.nn.sigmoid(_dot(_rms(x, g_ref[...]), wg_ref[...]))
    y = x + gate * _dot(p_ref[...], wp_ref[...])
    o_ref[...] = _rms(y, fg_ref[...]) if final else y


def _ple_layer(x2, p2, g, w_g, w_p, final_g, final):
    n, d = x2.shape
    t = ROW_BLOCK
    row = lambda width: pl.BlockSpec((t, width), lambda i: (i, 0))
    return pl.pallas_call(
        functools.partial(_ple_kernel, final=final),
        grid=(n // t,),
        in_specs=[row(d), row(p2.shape[1]), _full((1, d)), _full(w_g.shape), _full(w_p.shape), _full((1, d))],
        out_specs=row(d),
        out_shape=jax.ShapeDtypeStruct((n, d), F32),
        compiler_params=_params(("parallel",)),
        name="ple",
    )(x2, p2, g.reshape(1, d), w_g, w_p, final_g.reshape(1, d))


def kernel(x, p, positions, mix_norm, ffn_norm, pool_w, pool_scale, kv_in_norm, w_dkv, kv_norm, w_ukv, w_dq, q_norm, w_uq, w_o, peer_wq, peer_keys, peer_u, peer_v, ple_norm, ple_wg, ple_wp, final_norm):
    b, s, d = x.shape
    depth = p.shape[0]
    n_a = pool_w.shape[0]
    x2 = x.reshape(b * s, d)
    cos, sin = _rope_tables(positions)
    k_sh = v_sh = None
    for i in range(depth):
        if i < n_a:
            x2 = _pool_layer(x2, s, mix_norm[i], pool_w[i], pool_scale[i])
        else:
            if i == n_a:
                k_sh, v_sh = _shared_kv(x2, cos, sin, kv_in_norm, w_dkv, kv_norm, w_ukv)
            j = i - n_a
            q = _q_proj(x2, cos, sin, mix_norm[i], w_dq[j], q_norm[j], w_uq[j])
            o = _flash_attention(q, k_sh, v_sh, b, s)
            x2 = _out_proj(x2, o, w_o[j])
        x2 = _peer_layer(x2, ffn_norm[i], peer_wq[i], peer_keys[i], peer_u[i], peer_v[i])
        x2 = _ple_layer(x2, p[i].reshape(b * s, -1), ple_norm[i], ple_wg[i], ple_wp[i], final_norm, i == depth - 1)
    return x2.reshape(b, s, d)
```

```python
import functools
import math

import jax
import jax.numpy as jnp
from jax import lax
from jax.experimental import pallas as pl
from jax.experimental.pallas import tpu as pltpu

F32 = jnp.float32
HI = lax.Precision.HIGHEST
EPS = 1e-6

POOL_WINDOWS = (2, 4, 8, 16)
POOL_HALO = 16
MLA_HEADS = 8
QK_NOPE = 128
QK_ROPE = 64
V_HEAD = 128
KV_RANK = 256
ROPE_THETA = 10000.0
ATTN_SCALE = 1.0 / math.sqrt(QK_NOPE + QK_ROPE)
PEER_HEADS = 8
N_KEYS = 128
PEER_TOPK = 16
PEER_DHALF = 128

SUBLANES = 8
LANES = 128
ROW_BLOCK = 256
POOL_BLOCK = 512
ATTN_BLOCK = 512
ATTN_DTYPE = jnp.bfloat16
GATHER_SUB = 8
ISSUE_UNROLL = 8
VMEM_LIMIT = 56 * 1024 * 1024


def _params(sem):
    return pltpu.CompilerParams(dimension_semantics=sem, vmem_limit_bytes=VMEM_LIMIT)


def _rms(x, g):
    ms = jnp.mean(x * x, axis=-1, keepdims=True)
    return x * lax.rsqrt(ms + EPS) * g


def _dot(a, b):
    return jnp.dot(a, b, precision=HI, preferred_element_type=F32)


def _dot_nt(a, b):
    return lax.dot_general(a, b, (((1,), (1,)), ((), ())), precision=HI, preferred_element_type=F32)


def _full(shape):
    return pl.BlockSpec(shape, lambda *_: (0,) * len(shape))


def _pool_kernel(x_ref, halo_ref, g_ref, w_ref, sc_ref, o_ref, *, ts, blocks_per_seq):
    blk = pl.program_id(0) % blocks_per_seq
    x = x_ref[...]
    g = g_ref[...]
    h = _rms(x, g)
    hh = jnp.where(blk == 0, 0.0, _rms(halo_ref[...], g))
    full = jnp.concatenate([hh, h], axis=0)
    t = blk * ts + lax.broadcasted_iota(jnp.int32, (ts, 1), 0)
    gw = w_ref.shape[-1]
    outs = []
    for gi, w in enumerate(POOL_WINDOWS):
        p = full[:, gi * gw:(gi + 1) * gw]
        step = 1
        while step < w:
            p = p + pltpu.roll(p, step, axis=0)
            step *= 2
        win = p[POOL_HALO:, :]
        cnt = jnp.minimum(t + 1, w).astype(F32)
        pooled = win / cnt - h[:, gi * gw:(gi + 1) * gw]
        outs.append(_dot(pooled, w_ref[gi]))
    o_ref[...] = x + jnp.concatenate(outs, axis=-1) * sc_ref[...]


def _pool_layer(x2, seq, g, w, scale):
    n, d = x2.shape
    ts = min(POOL_BLOCK, seq)
    assert seq % ts == 0 and ts % POOL_HALO == 0 and all(wd & (wd - 1) == 0 and wd <= POOL_HALO for wd in POOL_WINDOWS)
    hb = ts // POOL_HALO
    return pl.pallas_call(
        functools.partial(_pool_kernel, ts=ts, blocks_per_seq=seq // ts),
        grid=(n // ts,),
        in_specs=[
            pl.BlockSpec((ts, d), lambda i: (i, 0)),
            pl.BlockSpec((POOL_HALO, d), lambda i: (jnp.maximum(i * hb - 1, 0), 0)),
            _full((1, d)),
            _full(w.shape),
            _full((1, d)),
        ],
        out_specs=pl.BlockSpec((ts, d), lambda i: (i, 0)),
        out_shape=jax.ShapeDtypeStruct((n, d), F32),
        compiler_params=_params(("parallel",)),
        name="pool_mixer",
    )(x2, x2, g.reshape(1, d), w, scale.reshape(1, d))


def _rope_kernel(pos_ref, inv_ref, cos_ref, sin_ref):
    ang = inv_ref[...] * pos_ref[0].astype(F32)
    cos_ref[0] = jnp.cos(ang)
    sin_ref[0] = jnp.sin(ang)


def _rope_tables(positions):
    b, s = positions.shape
    half = QK_ROPE // 2
    inv = ROPE_THETA ** (-jnp.arange(0, QK_ROPE, 2, dtype=F32) / QK_ROPE)
    cos_t, sin_t = pl.pallas_call(
        _rope_kernel,
        grid=(b,),
        in_specs=[pl.BlockSpec((1, 1, s), lambda i: (i, 0, 0)), _full((half, 1))],
        out_specs=[pl.BlockSpec((1, half, s), lambda i: (i, 0, 0))] * 2,
        out_shape=[jax.ShapeDtypeStruct((b, half, s), F32)] * 2,
        compiler_params=_params(("parallel",)),
        name="rope_tables",
    )(positions.reshape(b, 1, s), inv.reshape(half, 1))
    to_rows = lambda a: a.transpose(0, 2, 1).reshape(b * s, half)
    return to_rows(cos_t), to_rows(sin_t)


def _kv_kernel(x_ref, cos_ref, sin_ref, gin_ref, wdkv_ref, gkv_ref, wukv_ref, k_ref, v_ref):
    h = _rms(x_ref[...], gin_ref[...])
    ckv = _dot(h, wdkv_ref[...])
    c = _rms(ckv[:, :KV_RANK], gkv_ref[...])
    half = QK_ROPE // 2
    x1 = ckv[:, KV_RANK:KV_RANK + half]
    x2 = ckv[:, KV_RANK + half:]
    cos = cos_ref[...]
    sin = sin_ref[...]
    k1 = x1 * cos - x2 * sin
    k2 = x2 * cos + x1 * sin
    kv = _dot(c, wukv_ref[...])
    for hd in range(MLA_HEADS):
        k_ref[hd] = jnp.concatenate([kv[:, hd * QK_NOPE:(hd + 1) * QK_NOPE], k1, k2], axis=-1).astype(k_ref.dtype)
        off = MLA_HEADS * QK_NOPE + hd * V_HEAD
        v_ref[hd] = kv[:, off:off + V_HEAD].astype(v_ref.dtype)


def _split_heads(w, bounds):
    w3 = w.reshape(w.shape[0], MLA_HEADS, -1)
    return jnp.concatenate([w3[:, :, lo:hi].reshape(w.shape[0], -1) for lo, hi in bounds], axis=1)


def _shared_kv(x2, cos, sin, kv_in_norm, w_dkv, kv_norm, w_ukv):
    n, d = x2.shape
    t = ROW_BLOCK
    half = QK_ROPE // 2
    w_ukv_p = _split_heads(w_ukv, [(0, QK_NOPE), (QK_NOPE, QK_NOPE + V_HEAD)])
    row = lambda width: pl.BlockSpec((t, width), lambda i: (i, 0))
    return pl.pallas_call(
        _kv_kernel,
        grid=(n // t,),
        in_specs=[row(d), row(half), row(half), _full((1, d)), _full(w_dkv.shape), _full((1, KV_RANK)),
                  _full(w_ukv_p.shape)],
        out_specs=[pl.BlockSpec((MLA_HEADS, t, QK_NOPE + QK_ROPE), lambda i: (0, i, 0)),
                   pl.BlockSpec((MLA_HEADS, t, V_HEAD), lambda i: (0, i, 0))],
        out_shape=[jax.ShapeDtypeStruct((MLA_HEADS, n, QK_NOPE + QK_ROPE), ATTN_DTYPE),
                   jax.ShapeDtypeStruct((MLA_HEADS, n, V_HEAD), ATTN_DTYPE)],
        compiler_params=_params(("parallel",)),
        name="shared_kv",
    )(x2, cos, sin, kv_in_norm.reshape(1, d), w_dkv, kv_norm.reshape(1, KV_RANK), w_ukv_p)


def _q_kernel(x_ref, cos_ref, sin_ref, g_ref, wdq_ref, gq_ref, wuq_ref, q_ref):
    h = _rms(x_ref[...], g_ref[...])
    cq = _rms(_dot(h, wdq_ref[...]), gq_ref[...])
    q = _dot(cq, wuq_ref[...]) * ATTN_SCALE
    half = QK_ROPE // 2
    nope = MLA_HEADS * QK_NOPE
    cos = jnp.concatenate([cos_ref[...]] * MLA_HEADS, axis=-1)
    sin = jnp.concatenate([sin_ref[...]] * MLA_HEADS, axis=-1)
    p1 = q[:, nope:nope + MLA_HEADS * half]
    p2 = q[:, nope + MLA_HEADS * half:]
    r1 = p1 * cos - p2 * sin
    r2 = p2 * cos + p1 * sin
    for hd in range(MLA_HEADS):
        q_ref[hd] = jnp.concatenate([q[:, hd * QK_NOPE:(hd + 1) * QK_NOPE], r1[:, hd * half:(hd + 1) * half],
                                     r2[:, hd * half:(hd + 1) * half]], axis=-1).astype(q_ref.dtype)


def _q_proj(x2, cos, sin, g, w_dq, q_norm, w_uq):
    n, d = x2.shape
    t = ROW_BLOCK
    half = QK_ROPE // 2
    rank = w_dq.shape[1]
    w_uq_p = _split_heads(w_uq, [(0, QK_NOPE), (QK_NOPE, QK_NOPE + half), (QK_NOPE + half, QK_NOPE + QK_ROPE)])
    row = lambda width: pl.BlockSpec((t, width), lambda i: (i, 0))
    return pl.pallas_call(
        _q_kernel,
        grid=(n // t,),
        in_specs=[row(d), row(half), row(half), _full((1, d)), _full(w_dq.shape), _full((1, rank)),
                  _full(w_uq_p.shape)],
        out_specs=pl.BlockSpec((MLA_HEADS, t, QK_NOPE + QK_ROPE), lambda i: (0, i, 0)),
        out_shape=jax.ShapeDtypeStruct((MLA_HEADS, n, QK_NOPE + QK_ROPE), ATTN_DTYPE),
        compiler_params=_params(("parallel",)),
        name="mla_q",
    )(x2, cos, sin, g.reshape(1, d), w_dq, q_norm.reshape(1, rank), w_uq_p)


def _flash_kernel(q_ref, k_ref, v_ref, o_ref, m_sc, l_sc, acc_sc, *, tb):
    qi = pl.program_id(2)
    ki = pl.program_id(3)

    @pl.when(ki == 0)
    def _():
        m_sc[...] = jnp.full_like(m_sc, -jnp.inf)
        l_sc[...] = jnp.zeros_like(l_sc)
        acc_sc[...] = jnp.zeros_like(acc_sc)

    @pl.when(ki <= qi)
    def _():
        s = lax.dot_general(q_ref[0], k_ref[0], (((1,), (1,)), ((), ())), preferred_element_type=F32)
        qpos = qi * tb + lax.broadcasted_iota(jnp.int32, (tb, tb), 0)
        kpos = ki * tb + lax.broadcasted_iota(jnp.int32, (tb, tb), 1)
        s = jnp.where(kpos <= qpos, s, -jnp.inf)
        m_prev = m_sc[...]
        m_new = jnp.maximum(m_prev, jnp.max(s, axis=-1, keepdims=True))
        p = jnp.exp(s - m_new)
        alpha = jnp.exp(m_prev - m_new)
        l_sc[...] = alpha * l_sc[...] + jnp.sum(p, axis=-1, keepdims=True)
        acc_sc[...] = alpha * acc_sc[...] + jnp.dot(p.astype(v_ref.dtype), v_ref[0], preferred_element_type=F32)
        m_sc[...] = m_new

    @pl.when(ki == qi)
    def _():
        o_ref[...] = acc_sc[...] / l_sc[...]


def _flash_attention(q, k, v, batch, seq):
    tb = min(ATTN_BLOCK, seq)
    nb = seq // tb
    n = batch * seq
    dk = q.shape[-1]
    return pl.pallas_call(
        functools.partial(_flash_kernel, tb=tb),
        grid=(batch, MLA_HEADS, nb, nb),
        in_specs=[
            pl.BlockSpec((1, tb, dk), lambda b, h, qi, ki: (h, b * nb + qi, 0)),
            pl.BlockSpec((1, tb, dk), lambda b, h, qi, ki: (h, b * nb + jnp.minimum(ki, qi), 0)),
            pl.BlockSpec((1, tb, V_HEAD), lambda b, h, qi, ki: (h, b * nb + jnp.minimum(ki, qi), 0)),
        ],
        out_specs=pl.BlockSpec((tb, V_HEAD), lambda b, h, qi, ki: (b * nb + qi, h)),
        out_shape=jax.ShapeDtypeStruct((n, MLA_HEADS * V_HEAD), F32),
        scratch_shapes=[pltpu.VMEM((tb, 1), F32), pltpu.VMEM((tb, 1), F32), pltpu.VMEM((tb, V_HEAD), F32)],
        compiler_params=_params(("parallel", "parallel", "parallel", "arbitrary")),
        name="mla_flash",
    )(q, k, v)


def _oproj_kernel(x_ref, o_ref, w_ref, y_ref):
    y_ref[...] = x_ref[...] + _dot(o_ref[...], w_ref[...])


def _out_proj(x2, o, w_o):
    n, d = x2.shape
    t = ROW_BLOCK
    row = lambda width: pl.BlockSpec((t, width), lambda i: (i, 0))
    return pl.pallas_call(
        _oproj_kernel,
        grid=(n // t,),
        in_specs=[row(d), row(o.shape[1]), _full(w_o.shape)],
        out_specs=row(d),
        out_shape=jax.ShapeDtypeStruct((n, d), F32),
        compiler_params=_params(("parallel",)),
        name="mla_out",
    )(x2, o, w_o)


def _top_rows(problems, k):
    def pick(s, payload, r, vals, idxs):
        rows, t = s.shape
        iota = lax.broadcasted_iota(jnp.int32, (rows, t), 0)
        out_row = lax.broadcasted_iota(jnp.int32, (k, t), 0)
        m = jnp.max(s, axis=0, keepdims=True)
        pos = jnp.min(jnp.where(s == m, iota, rows), axis=0, keepdims=True)
        hit = iota == pos
        got = pos if payload is None else jnp.sum(jnp.where(hit, payload, 0), axis=0, keepdims=True)
        return (jnp.where(hit, -jnp.inf, s), jnp.where(out_row == r, m, vals), jnp.where(out_row == r, got, idxs))

    def body(r, carry):
        return tuple(pick(s, payload, r, vals, idxs) for (s, vals, idxs), (_, payload) in zip(carry, problems))

    t = problems[0][0].shape[1]
    init = tuple((s, jnp.zeros((k, t), F32), jnp.zeros((k, t), jnp.int32)) for s, _ in problems)
    return [(vals, idxs) for _, vals, idxs in lax.fori_loop(0, k, body, init)]


def _pair_candidates(s1, i1, s2, i2):
    neg = jnp.float32(-jnp.inf)
    sub = lax.broadcasted_iota(jnp.int32, (SUBLANES, s1.shape[1]), 0)
    cs, ci = [s1[0:1] + s2], [i1[0:1] * N_KEYS + i2]
    for a in range(1, PEER_TOPK // 2):
        keep = sub < PEER_TOPK // (a + 1)
        cs.append(jnp.where(keep, s1[a:a + 1] + s2[:SUBLANES], neg))
        ci.append(i1[a:a + 1] * N_KEYS + i2[:SUBLANES])
    cs.append(s1[PEER_TOPK // 2:] + s2[0:1])
    ci.append(i1[PEER_TOPK // 2:] * N_KEYS + i2[0:1])
    return jnp.concatenate(cs, axis=0), jnp.concatenate(ci, axis=0)


def _peer_front_kernel(x_ref, g_ref, wq_ref, keys_ref, hn_ref, idx_ref, gate_ref, *, lane_chunk):
    hn = _rms(x_ref[...], g_ref[...])

    @pl.when(pl.program_id(1) == 0)
    def _():
        hn_ref[...] = hn

    q = _dot(hn, wq_ref[...])
    cands = []
    for c in range(hn.shape[0] // lane_chunk):
        tok = slice(c * lane_chunk, (c + 1) * lane_chunk)
        scores = [_dot_nt(keys_ref[0, half], q[tok, half * PEER_DHALF:(half + 1) * PEER_DHALF]) for half in range(2)]
        (s1, i1), (s2, i2) = _top_rows([(sc, None) for sc in scores], PEER_TOPK)
        cands.append(_pair_candidates(s1, i1, s2, i2))
    for c, (best_s, best_i) in enumerate(_top_rows(cands, PEER_TOPK)):
        tok = slice(c * lane_chunk, (c + 1) * lane_chunk)
        e = jnp.exp(best_s - best_s[0:1, :])
        gate_ref[0, :, tok] = e / jnp.sum(e, axis=0, keepdims=True)
        idx_ref[0, :, tok] = best_i


def _peer_front(x2, g, w_q, keys):
    n, d = x2.shape
    t = ROW_BLOCK
    nb = n // t
    dq = 2 * PEER_DHALF
    k = PEER_HEADS * PEER_TOPK
    return pl.pallas_call(
        functools.partial(_peer_front_kernel, lane_chunk=128),
        grid=(nb, PEER_HEADS),
        in_specs=[
            pl.BlockSpec((t, d), lambda i, h: (i, 0)),
            _full((1, d)),
            pl.BlockSpec((d, dq), lambda i, h: (0, h)),
            pl.BlockSpec((1, 2, N_KEYS, PEER_DHALF), lambda i, h: (h, 0, 0, 0)),
        ],
        out_specs=[
            pl.BlockSpec((t, d), lambda i, h: (i, 0)),
            pl.BlockSpec((1, PEER_TOPK, t), lambda i, h: (i, h, 0)),
            pl.BlockSpec((1, PEER_TOPK, t), lambda i, h: (i, h, 0)),
        ],
        out_shape=[
            jax.ShapeDtypeStruct((n, d), F32),
            jax.ShapeDtypeStruct((nb, k, t), jnp.int32),
            jax.ShapeDtypeStruct((nb, k, t), F32),
        ],
        compiler_params=_params(("parallel", "arbitrary")),
        name="peer_front",
    )(x2, g.reshape(1, d), w_q, keys)


def _row_sums(p):
    k = p.shape[0]
    p = p.reshape(k // SUBLANES, SUBLANES, SUBLANES, LANES)
    s = lax.broadcasted_iota(jnp.int32, (1, 1, SUBLANES, LANES), 2)
    a, b = p[:, :4], p[:, 4:]
    m = (s & 4) == 0
    p = jnp.where(m, a, b) + pltpu.roll(jnp.where(m, b, a), 4, axis=2)
    for dist in (2, 1):
        h = p.shape[1] // 2
        a, b = p[:, :h], p[:, h:]
        m = (s & dist) == 0
        p = jnp.where(m, a + pltpu.roll(a, SUBLANES - dist, axis=2), b + pltpu.roll(b, dist, axis=2))
    return p.reshape(k, LANES)


def _peer_experts_kernel(idx_ref, hn_ref, gate_ref, x_ref, uv_hbm, o_ref, uvbuf, wbuf, sem, *, sub):
    t = hn_ref.shape[0]
    k = gate_ref.shape[1]
    n_sub = t // sub

    def row_copy(slot, e, j):
        return pltpu.make_async_copy(uv_hbm.at[e], uvbuf.at[slot, j], sem.at[slot])

    def issue(sb, slot):
        for tk in range(sub):
            def body(g, carry, tk=tk):
                base = ((sb * sub + tk) * k) + g * ISSUE_UNROLL
                for r in range(ISSUE_UNROLL):
                    e = idx_ref[base + r]
                    j = tk * k + g * ISSUE_UNROLL + r
                    row_copy(slot, e, j).start(priority=r % 2)
                return carry

            lax.fori_loop(0, k // ISSUE_UNROLL, body, 0)

    def wait(slot):
        pltpu.make_async_copy(uv_hbm.at[pl.ds(0, sub * k)], uvbuf.at[slot], sem.at[slot]).wait()

    lane = lax.broadcasted_iota(jnp.int32, (1, t), 1)

    def compute(sb, slot):
        for j in range(sub):
            tok = sb * sub + j
            prod = uvbuf[slot, j * k:(j + 1) * k, :SUBLANES] * hn_ref[tok][None]
            a = jnp.sum(_row_sums(prod), axis=-1, keepdims=True)
            gcol = jnp.sum(jnp.where(lane == tok, gate_ref[0], 0.0), axis=-1, keepdims=True)
            wbuf[j] = jnp.broadcast_to(gcol * jax.nn.gelu(a), (k, LANES))
            acc = [None] * 4
            for r in range(k):
                term = jnp.broadcast_to(wbuf[j, r:r + 1, :], (SUBLANES, LANES)) * uvbuf[slot, j * k + r, SUBLANES:]
                acc[r % 4] = term if acc[r % 4] is None else acc[r % 4] + term
            o_ref[tok] = x_ref[tok] + ((acc[0] + acc[1]) + (acc[2] + acc[3]))

    issue(0, 0)

    def step(sb, carry):
        slot = sb % 2

        @pl.when(sb + 1 < n_sub)
        def _():
            issue(sb + 1, 1 - slot)

        wait(slot)
        compute(sb, slot)
        return carry

    lax.fori_loop(0, n_sub, step, 0)


def _peer_experts(x2, hn, idx_flat, gate_t, uv):
    n, d = x2.shape
    nb, k, t = gate_t.shape
    sub = GATHER_SUB
    assert t % sub == 0 and k % SUBLANES == 0 and d == SUBLANES * LANES
    as_tiles = lambda a: a.reshape(-1, SUBLANES, LANES)
    row = pl.BlockSpec((t, SUBLANES, LANES), lambda i: (i, 0, 0))
    out = pl.pallas_call(
        functools.partial(_peer_experts_kernel, sub=sub),
        grid=(nb,),
        in_specs=[
            pl.BlockSpec((t * k,), lambda i: (i,), memory_space=pltpu.SMEM),
            row,
            pl.BlockSpec((1, k, t), lambda i: (i, 0, 0)),
            row,
            pl.BlockSpec(memory_space=pl.ANY),
        ],
        out_specs=row,
        out_shape=jax.ShapeDtypeStruct((n, SUBLANES, LANES), F32),
        scratch_shapes=[
            pltpu.VMEM((2, sub * k, 2 * SUBLANES, LANES), F32),
            pltpu.VMEM((sub, k, LANES), F32),
            pltpu.SemaphoreType.DMA((2,)),
        ],
        compiler_params=_params(("arbitrary",)),
        name="peer_experts",
    )(idx_flat, as_tiles(hn), gate_t, as_tiles(x2), uv)
    return out.reshape(n, d)


def _peer_layer(x2, g, w_q, keys, u, v):
    hn, idx_t, gate_t = _peer_front(x2, g, w_q, keys)
    nb, k, t = idx_t.shape
    idx_flat = idx_t.transpose(0, 2, 1).reshape(nb * t * k)
    uv = jnp.concatenate([u.reshape(-1, SUBLANES, LANES), v.reshape(-1, SUBLANES, LANES)], axis=1)
    return _peer_experts(x2, hn, idx_flat, gate_t, uv)


def _ple_kernel(x_ref, p_ref, g_ref, wg_ref, wp_ref, fg_ref, o_ref, *, final):
    x = x_ref[...]
    gate = jax.nn.sigmoid(_dot(_rms(x, g_ref[...]), wg_ref[...]))
    y = x + gate * _dot(p_ref[...], wp_ref[...])
    o_ref[...] = _rms(y, fg_ref[...]) if final else y


def _ple_layer(x2, p2, g, w_g, w_p, final_g, final):
    n, d = x2.shape
    t = ROW_BLOCK
    row = lambda width: pl.BlockSpec((t, width), lambda i: (i, 0))
    return pl.pallas_call(
        functools.partial(_ple_kernel, final=final),
        grid=(n // t,),
        in_specs=[row(d), row(p2.shape[1]), _full((1, d)), _full(w_g.shape), _full(w_p.shape), _full((1, d))],
        out_specs=row(d),
        out_shape=jax.ShapeDtypeStruct((n, d), F32),
        compiler_params=_params(("parallel",)),
        name="ple",
    )(x2, p2, g.reshape(1, d), w_g, w_p, final_g.reshape(1, d))


def kernel(x, p, positions, mix_norm, ffn_norm, pool_w, pool_scale, kv_in_norm, w_dkv, kv_norm, w_ukv, w_dq, q_norm, w_uq, w_o, peer_wq, peer_keys, peer_u, peer_v, ple_norm, ple_wg, ple_wp, final_norm):
    b, s, d = x.shape
    depth = p.shape[0]
    n_a = pool_w.shape[0]
    x2 = x.reshape(b * s, d)
    cos, sin = _rope_tables(positions)
    k_sh = v_sh = None
    for i in range(depth):
        if i < n_a:
            x2 = _pool_layer(x2, s, mix_norm[i], pool_w[i], pool_scale[i])
        else:
            if i == n_a:
                k_sh, v_sh = _shared_kv(x2, cos, sin, kv_in_norm, w_dkv, kv_norm, w_ukv)
            j = i - n_a
            q = _q_proj(x2, cos, sin, mix_norm[i], w_dq[j], q_norm[j], w_uq[j])
            o = _flash_attention(q, k_sh, v_sh, b, s)
            x2 = _out_proj(x2, o, w_o[j])
        x2 = _peer_layer(x2, ffn_norm[i], peer_wq[i], peer_keys[i], peer_u[i], peer_v[i])
        x2 = _ple_layer(x2, p[i].reshape(b * s, -1), ple_norm[i], ple_wg[i], ple_wp[i], final_norm, i == depth - 1)
    return x2.reshape(b, s, d)
```

```python
import functools
import math

import jax
import jax.numpy as jnp
from jax import lax
from jax.experimental import pallas as pl
from jax.experimental.pallas import tpu as pltpu

F32 = jnp.float32
BF16 = jnp.bfloat16
EPS = 1e-6

POOL_WINDOWS = (2, 4, 8, 16)
POOL_HALO = 16
MLA_HEADS = 8
QK_NOPE = 128
QK_ROPE = 64
V_HEAD = 128
KV_RANK = 256
ROPE_THETA = 10000.0
ATTN_SCALE = 1.0 / math.sqrt(QK_NOPE + QK_ROPE)
PEER_HEADS = 8
N_KEYS = 128
PEER_TOPK = 16
PEER_DHALF = 128

SUBLANES = 8
LANES = 128
ROW_BLOCK = 256
POOL_BLOCK = 512
ATTN_BLOCK = 512
ATTN_DTYPE = jnp.bfloat16
GATHER_SUB = 8
VMEM_LIMIT = 56 * 1024 * 1024


def _params(sem):
    return pltpu.CompilerParams(dimension_semantics=sem, vmem_limit_bytes=VMEM_LIMIT)


def _rms(x, g):
    ms = jnp.mean(x * x, axis=-1, keepdims=True)
    return x * lax.rsqrt(ms + EPS) * g


def _dot(a, w):
    return jnp.dot(a.astype(w.dtype), w, preferred_element_type=F32)


def _split(a):
    hi = a.astype(BF16)
    return hi, (a - hi.astype(F32)).astype(BF16)


def _dot3(a, b, dims):
    mm = lambda x, y: lax.dot_general(x, y, (dims, ((), ())), preferred_element_type=F32)
    return mm(a[0], b[0]) + (mm(a[0], b[1]) + mm(a[1], b[0]))


def _full(shape):
    return pl.BlockSpec(shape, lambda *_: (0,) * len(shape))


def _pool_kernel(x_ref, halo_ref, g_ref, w_ref, sc_ref, o_ref, *, ts, blocks_per_seq):
    blk = pl.program_id(0) % blocks_per_seq
    x = x_ref[...]
    g = g_ref[...]
    h = _rms(x, g)
    hh = jnp.where(blk == 0, 0.0, _rms(halo_ref[...], g))
    full = jnp.concatenate([hh, h], axis=0)
    t = blk * ts + lax.broadcasted_iota(jnp.int32, (ts, 1), 0)
    gw = w_ref.shape[-1]
    outs = []
    for gi, w in enumerate(POOL_WINDOWS):
        p = full[:, gi * gw:(gi + 1) * gw]
        step = 1
        while step < w:
            p = p + pltpu.roll(p, step, axis=0)
            step *= 2
        win = p[POOL_HALO:, :]
        cnt = jnp.minimum(t + 1, w).astype(F32)
        pooled = win / cnt - h[:, gi * gw:(gi + 1) * gw]
        outs.append(_dot(pooled, w_ref[gi]))
    o_ref[...] = x + jnp.concatenate(outs, axis=-1) * sc_ref[...]


def _pool_layer(x2, seq, g, w, scale):
    n, d = x2.shape
    ts = min(POOL_BLOCK, seq)
    assert seq % ts == 0 and ts % POOL_HALO == 0 and all(wd & (wd - 1) == 0 and wd <= POOL_HALO for wd in POOL_WINDOWS)
    hb = ts // POOL_HALO
    return pl.pallas_call(
        functools.partial(_pool_kernel, ts=ts, blocks_per_seq=seq // ts),
        grid=(n // ts,),
        in_specs=[
            pl.BlockSpec((ts, d), lambda i: (i, 0)),
            pl.BlockSpec((POOL_HALO, d), lambda i: (jnp.maximum(i * hb - 1, 0), 0)),
            _full((1, d)),
            _full(w.shape),
            _full((1, d)),
        ],
        out_specs=pl.BlockSpec((ts, d), lambda i: (i, 0)),
        out_shape=jax.ShapeDtypeStruct((n, d), F32),
        compiler_params=_params(("parallel",)),
        name="pool_mixer",
    )(x2, x2, g.reshape(1, d), w.astype(BF16), scale.reshape(1, d))


def _rope_kernel(pos_ref, inv_ref, cos_ref, sin_ref):
    ang = inv_ref[...] * pos_ref[0].astype(F32)
    cos_ref[0] = jnp.cos(ang)
    sin_ref[0] = jnp.sin(ang)


def _rope_tables(positions):
    b, s = positions.shape
    half = QK_ROPE // 2
    inv = ROPE_THETA ** (-jnp.arange(0, QK_ROPE, 2, dtype=F32) / QK_ROPE)
    cos_t, sin_t = pl.pallas_call(
        _rope_kernel,
        grid=(b,),
        in_specs=[pl.BlockSpec((1, 1, s), lambda i: (i, 0, 0)), _full((half, 1))],
        out_specs=[pl.BlockSpec((1, half, s), lambda i: (i, 0, 0))] * 2,
        out_shape=[jax.ShapeDtypeStruct((b, half, s), F32)] * 2,
        compiler_params=_params(("parallel",)),
        name="rope_tables",
    )(positions.reshape(b, 1, s), inv.reshape(half, 1))
    to_rows = lambda a: a.transpose(0, 2, 1).reshape(b * s, half)
    return to_rows(cos_t), to_rows(sin_t)


def _kv_kernel(x_ref, cos_ref, sin_ref, gin_ref, wdkv_ref, gkv_ref, wukv_ref, k_ref, v_ref):
    h = _rms(x_ref[...], gin_ref[...])
    ckv = _dot(h, wdkv_ref[...])
    c = _rms(ckv[:, :KV_RANK], gkv_ref[...])
    half = QK_ROPE // 2
    x1 = ckv[:, KV_RANK:KV_RANK + half]
    x2 = ckv[:, KV_RANK + half:]
    cos = cos_ref[...]
    sin = sin_ref[...]
    k1 = x1 * cos - x2 * sin
    k2 = x2 * cos + x1 * sin
    kv = _dot(c, wukv_ref[...])
    for hd in range(MLA_HEADS):
        k_ref[hd] = jnp.concatenate([kv[:, hd * QK_NOPE:(hd + 1) * QK_NOPE], k1, k2], axis=-1).astype(k_ref.dtype)
        off = MLA_HEADS * QK_NOPE + hd * V_HEAD
        v_ref[hd] = kv[:, off:off + V_HEAD].astype(v_ref.dtype)


def _split_heads(w, bounds):
    w3 = w.reshape(w.shape[0], MLA_HEADS, -1)
    return jnp.concatenate([w3[:, :, lo:hi].reshape(w.shape[0], -1) for lo, hi in bounds], axis=1)


def _shared_kv(x2, cos, sin, kv_in_norm, w_dkv, kv_norm, w_ukv):
    n, d = x2.shape
    t = ROW_BLOCK
    half = QK_ROPE // 2
    w_ukv_p = _split_heads(w_ukv, [(0, QK_NOPE), (QK_NOPE, QK_NOPE + V_HEAD)])
    row = lambda width: pl.BlockSpec((t, width), lambda i: (i, 0))
    return pl.pallas_call(
        _kv_kernel,
        grid=(n // t,),
        in_specs=[row(d), row(half), row(half), _full((1, d)), _full(w_dkv.shape), _full((1, KV_RANK)),
                  _full(w_ukv_p.shape)],
        out_specs=[pl.BlockSpec((MLA_HEADS, t, QK_NOPE + QK_ROPE), lambda i: (0, i, 0)),
                   pl.BlockSpec((MLA_HEADS, t, V_HEAD), lambda i: (0, i, 0))],
        out_shape=[jax.ShapeDtypeStruct((MLA_HEADS, n, QK_NOPE + QK_ROPE), ATTN_DTYPE),
                   jax.ShapeDtypeStruct((MLA_HEADS, n, V_HEAD), ATTN_DTYPE)],
        compiler_params=_params(("parallel",)),
        name="shared_kv",
    )(x2, cos, sin, kv_in_norm.reshape(1, d), w_dkv.astype(BF16), kv_norm.reshape(1, KV_RANK), w_ukv_p.astype(BF16))


def _q_kernel(x_ref, cos_ref, sin_ref, g_ref, wdq_ref, gq_ref, wuq_ref, q_ref):
    h = _rms(x_ref[...], g_ref[...])
    cq = _rms(_dot(h, wdq_ref[...]), gq_ref[...])
    q = _dot(cq, wuq_ref[...]) * ATTN_SCALE
    half = QK_ROPE // 2
    nope = MLA_HEADS * QK_NOPE
    cos = jnp.concatenate([cos_ref[...]] * MLA_HEADS, axis=-1)
    sin = jnp.concatenate([sin_ref[...]] * MLA_HEADS, axis=-1)
    p1 = q[:, nope:nope + MLA_HEADS * half]
    p2 = q[:, nope + MLA_HEADS * half:]
    r1 = p1 * cos - p2 * sin
    r2 = p2 * cos + p1 * sin
    for hd in range(MLA_HEADS):
        q_ref[hd] = jnp.concatenate([q[:, hd * QK_NOPE:(hd + 1) * QK_NOPE], r1[:, hd * half:(hd + 1) * half],
                                     r2[:, hd * half:(hd + 1) * half]], axis=-1).astype(q_ref.dtype)


def _q_proj(x2, cos, sin, g, w_dq, q_norm, w_uq):
    n, d = x2.shape
    t = ROW_BLOCK
    half = QK_ROPE // 2
    rank = w_dq.shape[1]
    w_uq_p = _split_heads(w_uq, [(0, QK_NOPE), (QK_NOPE, QK_NOPE + half), (QK_NOPE + half, QK_NOPE + QK_ROPE)])
    row = lambda width: pl.BlockSpec((t, width), lambda i: (i, 0))
    return pl.pallas_call(
        _q_kernel,
        grid=(n // t,),
        in_specs=[row(d), row(half), row(half), _full((1, d)), _full(w_dq.shape), _full((1, rank)),
                  _full(w_uq_p.shape)],
        out_specs=pl.BlockSpec((MLA_HEADS, t, QK_NOPE + QK_ROPE), lambda i: (0, i, 0)),
        out_shape=jax.ShapeDtypeStruct((MLA_HEADS, n, QK_NOPE + QK_ROPE), ATTN_DTYPE),
        compiler_params=_params(("parallel",)),
        name="mla_q",
    )(x2, cos, sin, g.reshape(1, d), w_dq.astype(BF16), q_norm.reshape(1, rank), w_uq_p.astype(BF16))


def _flash_kernel(q_ref, k_ref, v_ref, o_ref, m_sc, l_sc, acc_sc, *, tb):
    qi = pl.program_id(2)
    ki = pl.program_id(3)

    @pl.when(ki == 0)
    def _():
        m_sc[...] = jnp.full_like(m_sc, -jnp.inf)
        l_sc[...] = jnp.zeros_like(l_sc)
        acc_sc[...] = jnp.zeros_like(acc_sc)

    def update(masked):
        s = lax.dot_general(q_ref[0], k_ref[0], (((1,), (1,)), ((), ())), preferred_element_type=F32)
        if masked:
            row = lax.broadcasted_iota(jnp.int32, (tb, tb), 0)
            col = lax.broadcasted_iota(jnp.int32, (tb, tb), 1)
            s = jnp.where(col <= row, s, -jnp.inf)
        m_prev = m_sc[...]
        m_new = jnp.maximum(m_prev, jnp.max(s, axis=-1, keepdims=True))
        p = jnp.exp(s - m_new)
        alpha = jnp.exp(m_prev - m_new)
        l_sc[...] = alpha * l_sc[...] + jnp.sum(p, axis=-1, keepdims=True)
        acc_sc[...] = alpha * acc_sc[...] + jnp.dot(p.astype(v_ref.dtype), v_ref[0], preferred_element_type=F32)
        m_sc[...] = m_new

    pl.when(ki < qi)(functools.partial(update, False))
    pl.when(ki == qi)(functools.partial(update, True))

    @pl.when(ki == qi)
    def _():
        o_ref[...] = (acc_sc[...] / l_sc[...]).astype(o_ref.dtype)


def _flash_attention(q, k, v, batch, seq):
    tb = min(ATTN_BLOCK, seq)
    nb = seq // tb
    n = batch * seq
    dk = q.shape[-1]
    return pl.pallas_call(
        functools.partial(_flash_kernel, tb=tb),
        grid=(batch, MLA_HEADS, nb, nb),
        in_specs=[
            pl.BlockSpec((1, tb, dk), lambda b, h, qi, ki: (h, b * nb + qi, 0)),
            pl.BlockSpec((1, tb, dk), lambda b, h, qi, ki: (h, b * nb + jnp.minimum(ki, qi), 0)),
            pl.BlockSpec((1, tb, V_HEAD), lambda b, h, qi, ki: (h, b * nb + jnp.minimum(ki, qi), 0)),
        ],
        out_specs=pl.BlockSpec((tb, V_HEAD), lambda b, h, qi, ki: (b * nb + qi, h)),
        out_shape=jax.ShapeDtypeStruct((n, MLA_HEADS * V_HEAD), ATTN_DTYPE),
        scratch_shapes=[pltpu.VMEM((tb, 1), F32), pltpu.VMEM((tb, 1), F32), pltpu.VMEM((tb, V_HEAD), F32)],
        compiler_params=_params(("parallel", "parallel", "parallel", "arbitrary")),
        name="mla_flash",
    )(q, k, v)


def _oproj_kernel(x_ref, o_ref, w_ref, y_ref):
    y_ref[...] = x_ref[...] + _dot(o_ref[...], w_ref[...])


def _out_proj(x2, o, w_o):
    n, d = x2.shape
    t = ROW_BLOCK
    row = lambda width: pl.BlockSpec((t, width), lambda i: (i, 0))
    return pl.pallas_call(
        _oproj_kernel,
        grid=(n // t,),
        in_specs=[row(d), row(o.shape[1]), _full(w_o.shape)],
        out_specs=row(d),
        out_shape=jax.ShapeDtypeStruct((n, d), F32),
        compiler_params=_params(("parallel",)),
        name="mla_out",
    )(x2, o, w_o.astype(BF16))


def _top_rows(problems, k):
    def pick(s, payload, r, vals, idxs):
        rows, t = s.shape
        iota = lax.broadcasted_iota(jnp.int32, (rows, t), 0)
        out_row = lax.broadcasted_iota(jnp.int32, (k, t), 0)
        m = jnp.max(s, axis=0, keepdims=True)
        pos = jnp.min(jnp.where(s == m, iota, rows), axis=0, keepdims=True)
        hit = iota == pos
        got = pos if payload is None else jnp.sum(jnp.where(hit, payload, 0), axis=0, keepdims=True)
        return (jnp.where(hit, -jnp.inf, s), jnp.where(out_row == r, m, vals), jnp.where(out_row == r, got, idxs))

    def body(r, carry):
        return tuple(pick(s, payload, r, vals, idxs) for (s, vals, idxs), (_, payload) in zip(carry, problems))

    t = problems[0][0].shape[1]
    init = tuple((s, jnp.zeros((k, t), F32), jnp.zeros((k, t), jnp.int32)) for s, _ in problems)
    return [(vals, idxs) for _, vals, idxs in lax.fori_loop(0, k, body, init)]


def _pair_candidates(s1, i1, s2, i2):
    neg = jnp.float32(-jnp.inf)
    sub = lax.broadcasted_iota(jnp.int32, (SUBLANES, s1.shape[1]), 0)
    cs, ci = [s1[0:1] + s2], [i1[0:1] * N_KEYS + i2]
    for a in range(1, PEER_TOPK // 2):
        keep = sub < PEER_TOPK // (a + 1)
        cs.append(jnp.where(keep, s1[a:a + 1] + s2[:SUBLANES], neg))
        ci.append(i1[a:a + 1] * N_KEYS + i2[:SUBLANES])
    cs.append(s1[PEER_TOPK // 2:] + s2[0:1])
    ci.append(i1[PEER_TOPK // 2:] * N_KEYS + i2[0:1])
    return jnp.concatenate(cs, axis=0), jnp.concatenate(ci, axis=0)


def _peer_front_kernel(x_ref, g_ref, wq_hi_ref, wq_lo_ref, keys_hi_ref, keys_lo_ref, hn_ref, idx_ref, gate_ref, *,
                       lane_chunk):
    hn = _rms(x_ref[...], g_ref[...])

    @pl.when(pl.program_id(1) == 0)
    def _():
        hn_ref[...] = hn

    q = _dot3(_split(hn), (wq_hi_ref[...], wq_lo_ref[...]), ((1,), (0,)))
    cands = []
    for c in range(hn.shape[0] // lane_chunk):
        tok = slice(c * lane_chunk, (c + 1) * lane_chunk)
        scores = [_dot3((keys_hi_ref[0, half], keys_lo_ref[0, half]),
                        _split(q[tok, half * PEER_DHALF:(half + 1) * PEER_DHALF]), ((1,), (1,))) for half in range(2)]
        (s1, i1), (s2, i2) = _top_rows([(sc, None) for sc in scores], PEER_TOPK)
        cands.append(_pair_candidates(s1, i1, s2, i2))
    for c, (best_s, best_i) in enumerate(_top_rows(cands, PEER_TOPK)):
        tok = slice(c * lane_chunk, (c + 1) * lane_chunk)
        e = jnp.exp(best_s - best_s[0:1, :])
        gate_ref[0, :, tok] = e / jnp.sum(e, axis=0, keepdims=True)
        idx_ref[0, :, tok] = best_i


def _peer_front(x2, g, w_q, keys):
    n, d = x2.shape
    t = ROW_BLOCK
    nb = n // t
    dq = 2 * PEER_DHALF
    k = PEER_HEADS * PEER_TOPK
    return pl.pallas_call(
        functools.partial(_peer_front_kernel, lane_chunk=128),
        grid=(nb, PEER_HEADS),
        in_specs=[
            pl.BlockSpec((t, d), lambda i, h: (i, 0)),
            _full((1, d)),
            pl.BlockSpec((d, dq), lambda i, h: (0, h)),
            pl.BlockSpec((d, dq), lambda i, h: (0, h)),
            pl.BlockSpec((1, 2, N_KEYS, PEER_DHALF), lambda i, h: (h, 0, 0, 0)),
            pl.BlockSpec((1, 2, N_KEYS, PEER_DHALF), lambda i, h: (h, 0, 0, 0)),
        ],
        out_specs=[
            pl.BlockSpec((t, d), lambda i, h: (i, 0)),
            pl.BlockSpec((1, PEER_TOPK, t), lambda i, h: (i, h, 0)),
            pl.BlockSpec((1, PEER_TOPK, t), lambda i, h: (i, h, 0)),
        ],
        out_shape=[
            jax.ShapeDtypeStruct((n, d), F32),
            jax.ShapeDtypeStruct((nb, k, t), jnp.int32),
            jax.ShapeDtypeStruct((nb, k, t), F32),
        ],
        compiler_params=_params(("parallel", "arbitrary")),
        name="peer_front",
    )(x2, g.reshape(1, d), *_split(w_q), *_split(keys))


def _row_sums(p):
    k = p.shape[0]
    p = p.reshape(k // SUBLANES, SUBLANES, SUBLANES, LANES)
    s = lax.broadcasted_iota(jnp.int32, (1, 1, SUBLANES, LANES), 2)
    a, b = p[:, :4], p[:, 4:]
    m = (s & 4) == 0
    p = jnp.where(m, a, b) + pltpu.roll(jnp.where(m, b, a), 4, axis=2)
    for dist in (2, 1):
        h = p.shape[1] // 2
        a, b = p[:, :h], p[:, h:]
        m = (s & dist) == 0
        p = jnp.where(m, a + pltpu.roll(a, SUBLANES - dist, axis=2), b + pltpu.roll(b, dist, axis=2))
    return p.reshape(k, LANES)


def _peer_experts_kernel(idx_ref, idx_next_ref, hn_ref, gate_ref, x_ref, uv_hbm, o_ref, buf_a, buf_b, wbuf, sem, *, sub):
    step = pl.program_id(0)
    last_step = pl.num_programs(0) - 1
    t = hn_ref.shape[0]
    k = gate_ref.shape[1]
    n_sub = t // sub
    bufs = (buf_a, buf_b)
    lane = lax.broadcasted_iota(jnp.int32, (1, t), 1)
    hi_mask = jnp.int32(-65536)

    def issue_token(idx, base, tk, slot):
        for r in range(k):
            e = idx[base + tk * k + r]
            pltpu.make_async_copy(uv_hbm.at[e], bufs[slot].at[tk * k + r], sem.at[slot]).start(priority=r % 2)

    def wait(slot):
        pltpu.make_async_copy(uv_hbm.at[pl.ds(0, sub * k)], bufs[slot], sem.at[slot]).wait()

    def compute_token(sb, tk, slot):
        buf = bufs[slot]
        tok = sb * sub + tk
        u = lax.bitcast_convert_type(buf[pl.ds(tk * k, k)] << 16, F32)
        a = jnp.sum(_row_sums(u * hn_ref[tok][None]), axis=-1, keepdims=True)
        gcol = jnp.sum(jnp.where(lane == tok, gate_ref[0], 0.0), axis=-1, keepdims=True)
        wbuf[tk] = jnp.broadcast_to(gcol * jax.nn.gelu(a), (k, LANES))
        acc = [None] * 4
        for r in range(k):
            v = lax.bitcast_convert_type(buf[tk * k + r] & hi_mask, F32)
            term = jnp.broadcast_to(wbuf[tk, r:r + 1, :], (SUBLANES, LANES)) * v
            acc[r % 4] = term if acc[r % 4] is None else acc[r % 4] + term
        o_ref[tok] = x_ref[tok] + ((acc[0] + acc[1]) + (acc[2] + acc[3]))

    def run_sub(sb, slot, next_idx, next_base):
        wait(slot)

        def body(tk, carry):
            if next_idx is not None:
                issue_token(next_idx, next_base, tk, 1 - slot)
            compute_token(sb, tk, slot)
            return carry

        lax.fori_loop(0, sub, body, 0)

    @pl.when(step == 0)
    def _():
        lax.fori_loop(0, sub, lambda tk, c: (issue_token(idx_ref, 0, tk, 0), c)[1], 0)

    def pair(sp, carry):
        run_sub(2 * sp, 0, idx_ref, (2 * sp + 1) * sub * k)
        run_sub(2 * sp + 1, 1, idx_ref, (2 * sp + 2) * sub * k)
        return carry

    lax.fori_loop(0, n_sub // 2 - 1, pair, 0)
    run_sub(n_sub - 2, 0, idx_ref, (n_sub - 1) * sub * k)

    @pl.when(step < last_step)
    def _():
        run_sub(n_sub - 1, 1, idx_next_ref, 0)

    @pl.when(step == last_step)
    def _():
        run_sub(n_sub - 1, 1, None, 0)


def _peer_experts(x2, hn, idx_flat, gate_t, uv):
    n, d = x2.shape
    nb, k, t = gate_t.shape
    sub = GATHER_SUB
    assert t % (2 * sub) == 0 and k % SUBLANES == 0 and d == SUBLANES * LANES
    as_tiles = lambda a: a.reshape(-1, SUBLANES, LANES)
    row = pl.BlockSpec((t, SUBLANES, LANES), lambda i: (i, 0, 0))
    out = pl.pallas_call(
        functools.partial(_peer_experts_kernel, sub=sub),
        grid=(nb,),
        in_specs=[
            pl.BlockSpec((t * k,), lambda i: (i,), memory_space=pltpu.SMEM),
            pl.BlockSpec((t * k,), lambda i: (jnp.minimum(i + 1, nb - 1),), memory_space=pltpu.SMEM),
            row,
            pl.BlockSpec((1, k, t), lambda i: (i, 0, 0)),
            row,
            pl.BlockSpec(memory_space=pl.ANY),
        ],
        out_specs=row,
        out_shape=jax.ShapeDtypeStruct((n, SUBLANES, LANES), F32),
        scratch_shapes=[
            pltpu.VMEM((sub * k, SUBLANES, LANES), jnp.int32),
            pltpu.VMEM((sub * k, SUBLANES, LANES), jnp.int32),
            pltpu.VMEM((sub, k, LANES), F32),
            pltpu.SemaphoreType.DMA((2,)),
        ],
        compiler_params=_params(("arbitrary",)),
        name="peer_experts",
    )(idx_flat, idx_flat, as_tiles(hn), gate_t, as_tiles(x2), uv)
    return out.reshape(n, d)


def _pack_experts(u, v):
    bits = lambda a: lax.bitcast_convert_type(a.astype(jnp.bfloat16), jnp.uint16).astype(jnp.uint32)
    words = (bits(v) << 16) | bits(u)
    return lax.bitcast_convert_type(words, jnp.int32).reshape(-1, SUBLANES, LANES)


def _peer_layer(x2, g, w_q, keys, u, v):
    hn, idx_t, gate_t = _peer_front(x2, g, w_q, keys)
    nb, k, t = idx_t.shape
    idx_flat = idx_t.transpose(0, 2, 1).reshape(nb * t * k)
    return _peer_experts(x2, hn, idx_flat, gate_t, _pack_experts(u, v))


def _ple_kernel(x_ref, p_ref, g_ref, wg_ref, wp_ref, fg_ref, o_ref, *, final):
    x = x_ref[...]
    gate = jax.nn.sigmoid(_dot(_rms(x, g_ref[...]), wg_ref[...]))
    y = x + gate * _dot(p_ref[...], wp_ref[...])
    o_ref[...] = _rms(y, fg_ref[...]) if final else y


def _ple_layer(x2, p2, g, w_g, w_p, final_g, final):
    n, d = x2.shape
    t = ROW_BLOCK
    row = lambda width: pl.BlockSpec((t, width), lambda i: (i, 0))
    return pl.pallas_call(
        functools.partial(_ple_kernel, final=final),
        grid=(n // t,),
        in_specs=[row(d), row(p2.shape[1]), _full((1, d)), _full(w_g.shape), _full(w_p.shape), _full((1, d))],
        out_specs=row(d),
        out_shape=jax.ShapeDtypeStruct((n, d), F32),
        compiler_params=_params(("parallel",)),
        name="ple",
    )(x2, p2, g.reshape(1, d), w_g.astype(BF16), w_p.astype(BF16), final_g.reshape(1, d))


def kernel(x, p, positions, mix_norm, ffn_norm, pool_w, pool_scale, kv_in_norm, w_dkv, kv_norm, w_ukv, w_dq, q_norm, w_uq, w_o, peer_wq, peer_keys, peer_u, peer_v, ple_norm, ple_wg, ple_wp, final_norm):
    b, s, d = x.shape
    depth = p.shape[0]
    n_a = pool_w.shape[0]
    x2 = x.reshape(b * s, d)
    cos, sin = _rope_tables(positions)
    k_sh = v_sh = None
    for i in range(depth):
        if i < n_a:
            x2 = _pool_layer(x2, s, mix_norm[i], pool_w[i], pool_scale[i])
        else:
            if i == n_a:
                k_sh, v_sh = _shared_kv(x2, cos, sin, kv_in_norm, w_dkv, kv_norm, w_ukv)
            j = i - n_a
            q = _q_proj(x2, cos, sin, mix_norm[i], w_dq[j], q_norm[j], w_uq[j])
            o = _flash_attention(q, k_sh, v_sh, b, s)
            x2 = _out_proj(x2, o, w_o[j])
        x2 = _peer_layer(x2, ffn_norm[i], peer_wq[i], peer_keys[i], peer_u[i], peer_v[i])
        x2 = _ple_layer(x2, p[i].reshape(b * s, -1), ple_norm[i], ple_wg[i], ple_wp[i], final_norm, i == depth - 1)
    return x2.reshape(b, s, d)
```

```python
import functools
import math

import jax
import jax.numpy as jnp
from jax import lax
from jax.experimental import pallas as pl
from jax.experimental.pallas import tpu as pltpu
from jax.experimental.pallas import tpu_sc as plsc

F32 = jnp.float32
BF16 = jnp.bfloat16
EPS = 1e-6

POOL_WINDOWS = (2, 4, 8, 16)
POOL_HALO = 16
MLA_HEADS = 8
QK_NOPE = 128
QK_ROPE = 64
V_HEAD = 128
KV_RANK = 256
ROPE_THETA = 10000.0
ATTN_SCALE = 1.0 / math.sqrt(QK_NOPE + QK_ROPE)
PEER_HEADS = 8
N_KEYS = 128
PEER_TOPK = 16
PEER_DHALF = 128

SUBLANES = 8
LANES = 128
ROW_BLOCK = 256
POOL_BLOCK = 512
ATTN_BLOCK = 512
ATTN_DTYPE = jnp.bfloat16
GATHER_SUB = 8
VMEM_LIMIT = 56 * 1024 * 1024
SC_CORES = 2
SC_SUBCORES = 16
SC_LANES = 16
SC_TOKENS = 8
SC_U_ROWS = 16
SC_ACC_WORDS = 8
SC_SHARE_NUM, SC_SHARE_DEN = 11, 32


def _params(sem):
    return pltpu.CompilerParams(dimension_semantics=sem, vmem_limit_bytes=VMEM_LIMIT)


def _rms(x, g):
    ms = jnp.mean(x * x, axis=-1, keepdims=True)
    return x * lax.rsqrt(ms + EPS) * g


def _dot(a, w):
    return jnp.dot(a.astype(w.dtype), w, preferred_element_type=F32)


def _split(a):
    hi = a.astype(BF16)
    return hi, (a - hi.astype(F32)).astype(BF16)


def _dot3(a, b, dims):
    mm = lambda x, y: lax.dot_general(x, y, (dims, ((), ())), preferred_element_type=F32)
    return mm(a[0], b[0]) + (mm(a[0], b[1]) + mm(a[1], b[0]))


def _full(shape):
    return pl.BlockSpec(shape, lambda *_: (0,) * len(shape))


def _pool_kernel(x_ref, halo_ref, g_ref, w_ref, sc_ref, o_ref, *, ts, blocks_per_seq):
    blk = pl.program_id(0) % blocks_per_seq
    x = x_ref[...]
    g = g_ref[...]
    h = _rms(x, g)
    hh = jnp.where(blk == 0, 0.0, _rms(halo_ref[...], g))
    full = jnp.concatenate([hh, h], axis=0)
    t = blk * ts + lax.broadcasted_iota(jnp.int32, (ts, 1), 0)
    gw = w_ref.shape[-1]
    outs = []
    for gi, w in enumerate(POOL_WINDOWS):
        p = full[:, gi * gw:(gi + 1) * gw]
        step = 1
        while step < w:
            p = p + pltpu.roll(p, step, axis=0)
            step *= 2
        win = p[POOL_HALO:, :]
        cnt = jnp.minimum(t + 1, w).astype(F32)
        pooled = win / cnt - h[:, gi * gw:(gi + 1) * gw]
        outs.append(_dot(pooled, w_ref[gi]))
    o_ref[...] = x + jnp.concatenate(outs, axis=-1) * sc_ref[...]


def _pool_layer(x2, seq, g, w, scale):
    n, d = x2.shape
    ts = min(POOL_BLOCK, seq)
    assert seq % ts == 0 and ts % POOL_HALO == 0 and all(wd & (wd - 1) == 0 and wd <= POOL_HALO for wd in POOL_WINDOWS)
    hb = ts // POOL_HALO
    return pl.pallas_call(
        functools.partial(_pool_kernel, ts=ts, blocks_per_seq=seq // ts),
        grid=(n // ts,),
        in_specs=[
            pl.BlockSpec((ts, d), lambda i: (i, 0)),
            pl.BlockSpec((POOL_HALO, d), lambda i: (jnp.maximum(i * hb - 1, 0), 0)),
            _full((1, d)),
            _full(w.shape),
            _full((1, d)),
        ],
        out_specs=pl.BlockSpec((ts, d), lambda i: (i, 0)),
        out_shape=jax.ShapeDtypeStruct((n, d), F32),
        compiler_params=_params(("parallel",)),
        name="pool_mixer",
    )(x2, x2, g.reshape(1, d), w.astype(BF16), scale.reshape(1, d))


def _rope_kernel(pos_ref, inv_ref, cos_ref, sin_ref):
    ang = inv_ref[...] * pos_ref[0].astype(F32)
    cos_ref[0] = jnp.cos(ang)
    sin_ref[0] = jnp.sin(ang)


def _rope_tables(positions):
    b, s = positions.shape
    half = QK_ROPE // 2
    inv = ROPE_THETA ** (-jnp.arange(0, QK_ROPE, 2, dtype=F32) / QK_ROPE)
    cos_t, sin_t = pl.pallas_call(
        _rope_kernel,
        grid=(b,),
        in_specs=[pl.BlockSpec((1, 1, s), lambda i: (i, 0, 0)), _full((half, 1))],
        out_specs=[pl.BlockSpec((1, half, s), lambda i: (i, 0, 0))] * 2,
        out_shape=[jax.ShapeDtypeStruct((b, half, s), F32)] * 2,
        compiler_params=_params(("parallel",)),
        name="rope_tables",
    )(positions.reshape(b, 1, s), inv.reshape(half, 1))
    to_rows = lambda a: a.transpose(0, 2, 1).reshape(b * s, half)
    return to_rows(cos_t), to_rows(sin_t)


def _kv_kernel(x_ref, cos_ref, sin_ref, gin_ref, wdkv_ref, gkv_ref, wukv_ref, k_ref, v_ref):
    h = _rms(x_ref[...], gin_ref[...])
    ckv = _dot(h, wdkv_ref[...])
    c = _rms(ckv[:, :KV_RANK], gkv_ref[...])
    half = QK_ROPE // 2
    x1 = ckv[:, KV_RANK:KV_RANK + half]
    x2 = ckv[:, KV_RANK + half:]
    cos = cos_ref[...]
    sin = sin_ref[...]
    k1 = x1 * cos - x2 * sin
    k2 = x2 * cos + x1 * sin
    kv = _dot(c, wukv_ref[...])
    for hd in range(MLA_HEADS):
        k_ref[hd] = jnp.concatenate([kv[:, hd * QK_NOPE:(hd + 1) * QK_NOPE], k1, k2], axis=-1).astype(k_ref.dtype)
        off = MLA_HEADS * QK_NOPE + hd * V_HEAD
        v_ref[hd] = kv[:, off:off + V_HEAD].astype(v_ref.dtype)


def _split_heads(w, bounds):
    w3 = w.reshape(w.shape[0], MLA_HEADS, -1)
    return jnp.concatenate([w3[:, :, lo:hi].reshape(w.shape[0], -1) for lo, hi in bounds], axis=1)


def _shared_kv(x2, cos, sin, kv_in_norm, w_dkv, kv_norm, w_ukv):
    n, d = x2.shape
    t = ROW_BLOCK
    half = QK_ROPE // 2
    w_ukv_p = _split_heads(w_ukv, [(0, QK_NOPE), (QK_NOPE, QK_NOPE + V_HEAD)])
    row = lambda width: pl.BlockSpec((t, width), lambda i: (i, 0))
    return pl.pallas_call(
        _kv_kernel,
        grid=(n // t,),
        in_specs=[row(d), row(half), row(half), _full((1, d)), _full(w_dkv.shape), _full((1, KV_RANK)),
                  _full(w_ukv_p.shape)],
        out_specs=[pl.BlockSpec((MLA_HEADS, t, QK_NOPE + QK_ROPE), lambda i: (0, i, 0)),
                   pl.BlockSpec((MLA_HEADS, t, V_HEAD), lambda i: (0, i, 0))],
        out_shape=[jax.ShapeDtypeStruct((MLA_HEADS, n, QK_NOPE + QK_ROPE), ATTN_DTYPE),
                   jax.ShapeDtypeStruct((MLA_HEADS, n, V_HEAD), ATTN_DTYPE)],
        compiler_params=_params(("parallel",)),
        name="shared_kv",
    )(x2, cos, sin, kv_in_norm.reshape(1, d), w_dkv.astype(BF16), kv_norm.reshape(1, KV_RANK), w_ukv_p.astype(BF16))


def _q_kernel(x_ref, cos_ref, sin_ref, g_ref, wdq_ref, gq_ref, wuq_ref, q_ref):
    h = _rms(x_ref[...], g_ref[...])
    cq = _rms(_dot(h, wdq_ref[...]), gq_ref[...])
    q = _dot(cq, wuq_ref[...]) * ATTN_SCALE
    half = QK_ROPE // 2
    nope = MLA_HEADS * QK_NOPE
    cos = jnp.concatenate([cos_ref[...]] * MLA_HEADS, axis=-1)
    sin = jnp.concatenate([sin_ref[...]] * MLA_HEADS, axis=-1)
    p1 = q[:, nope:nope + MLA_HEADS * half]
    p2 = q[:, nope + MLA_HEADS * half:]
    r1 = p1 * cos - p2 * sin
    r2 = p2 * cos + p1 * sin
    for hd in range(MLA_HEADS):
        q_ref[hd] = jnp.concatenate([q[:, hd * QK_NOPE:(hd + 1) * QK_NOPE], r1[:, hd * half:(hd + 1) * half],
                                     r2[:, hd * half:(hd + 1) * half]], axis=-1).astype(q_ref.dtype)


def _q_proj(x2, cos, sin, g, w_dq, q_norm, w_uq):
    n, d = x2.shape
    t = ROW_BLOCK
    half = QK_ROPE // 2
    rank = w_dq.shape[1]
    w_uq_p = _split_heads(w_uq, [(0, QK_NOPE), (QK_NOPE, QK_NOPE + half), (QK_NOPE + half, QK_NOPE + QK_ROPE)])
    row = lambda width: pl.BlockSpec((t, width), lambda i: (i, 0))
    return pl.pallas_call(
        _q_kernel,
        grid=(n // t,),
        in_specs=[row(d), row(half), row(half), _full((1, d)), _full(w_dq.shape), _full((1, rank)),
                  _full(w_uq_p.shape)],
        out_specs=pl.BlockSpec((MLA_HEADS, t, QK_NOPE + QK_ROPE), lambda i: (0, i, 0)),
        out_shape=jax.ShapeDtypeStruct((MLA_HEADS, n, QK_NOPE + QK_ROPE), ATTN_DTYPE),
        compiler_params=_params(("parallel",)),
        name="mla_q",
    )(x2, cos, sin, g.reshape(1, d), w_dq.astype(BF16), q_norm.reshape(1, rank), w_uq_p.astype(BF16))


def _flash_kernel(q_ref, k_ref, v_ref, o_ref, m_sc, l_sc, acc_sc, *, tb):
    qi = pl.program_id(2)
    ki = pl.program_id(3)

    @pl.when(ki == 0)
    def _():
        m_sc[...] = jnp.full_like(m_sc, -jnp.inf)
        l_sc[...] = jnp.zeros_like(l_sc)
        acc_sc[...] = jnp.zeros_like(acc_sc)

    def update(masked):
        s = lax.dot_general(q_ref[0], k_ref[0], (((1,), (1,)), ((), ())), preferred_element_type=F32)
        if masked:
            row = lax.broadcasted_iota(jnp.int32, (tb, tb), 0)
            col = lax.broadcasted_iota(jnp.int32, (tb, tb), 1)
            s = jnp.where(col <= row, s, -jnp.inf)
        m_prev = m_sc[...]
        m_new = jnp.maximum(m_prev, jnp.max(s, axis=-1, keepdims=True))
        p = jnp.exp(s - m_new)
        alpha = jnp.exp(m_prev - m_new)
        l_sc[...] = alpha * l_sc[...] + jnp.sum(p, axis=-1, keepdims=True)
        acc_sc[...] = alpha * acc_sc[...] + jnp.dot(p.astype(v_ref.dtype), v_ref[0], preferred_element_type=F32)
        m_sc[...] = m_new

    pl.when(ki < qi)(functools.partial(update, False))
    pl.when(ki == qi)(functools.partial(update, True))

    @pl.when(ki == qi)
    def _():
        o_ref[...] = (acc_sc[...] / l_sc[...]).astype(o_ref.dtype)


def _flash_attention(q, k, v, batch, seq):
    tb = min(ATTN_BLOCK, seq)
    nb = seq // tb
    n = batch * seq
    dk = q.shape[-1]
    return pl.pallas_call(
        functools.partial(_flash_kernel, tb=tb),
        grid=(batch, MLA_HEADS, nb, nb),
        in_specs=[
            pl.BlockSpec((1, tb, dk), lambda b, h, qi, ki: (h, b * nb + qi, 0)),
            pl.BlockSpec((1, tb, dk), lambda b, h, qi, ki: (h, b * nb + jnp.minimum(ki, qi), 0)),
            pl.BlockSpec((1, tb, V_HEAD), lambda b, h, qi, ki: (h, b * nb + jnp.minimum(ki, qi), 0)),
        ],
        out_specs=pl.BlockSpec((tb, V_HEAD), lambda b, h, qi, ki: (b * nb + qi, h)),
        out_shape=jax.ShapeDtypeStruct((n, MLA_HEADS * V_HEAD), ATTN_DTYPE),
        scratch_shapes=[pltpu.VMEM((tb, 1), F32), pltpu.VMEM((tb, 1), F32), pltpu.VMEM((tb, V_HEAD), F32)],
        compiler_params=_params(("parallel", "parallel", "parallel", "arbitrary")),
        name="mla_flash",
    )(q, k, v)


def _oproj_kernel(x_ref, o_ref, w_ref, y_ref):
    y_ref[...] = x_ref[...] + _dot(o_ref[...], w_ref[...])


def _out_proj(x2, o, w_o):
    n, d = x2.shape
    t = ROW_BLOCK
    row = lambda width: pl.BlockSpec((t, width), lambda i: (i, 0))
    return pl.pallas_call(
        _oproj_kernel,
        grid=(n // t,),
        in_specs=[row(d), row(o.shape[1]), _full(w_o.shape)],
        out_specs=row(d),
        out_shape=jax.ShapeDtypeStruct((n, d), F32),
        compiler_params=_params(("parallel",)),
        name="mla_out",
    )(x2, o, w_o.astype(BF16))


def _top_rows(problems, k):
    def pick(s, payload, r, vals, idxs):
        rows, t = s.shape
        iota = lax.broadcasted_iota(jnp.int32, (rows, t), 0)
        out_row = lax.broadcasted_iota(jnp.int32, (k, t), 0)
        m = jnp.max(s, axis=0, keepdims=True)
        pos = jnp.min(jnp.where(s == m, iota, rows), axis=0, keepdims=True)
        hit = iota == pos
        got = pos if payload is None else jnp.sum(jnp.where(hit, payload, 0), axis=0, keepdims=True)
        return (jnp.where(hit, -jnp.inf, s), jnp.where(out_row == r, m, vals), jnp.where(out_row == r, got, idxs))

    def body(r, carry):
        return tuple(pick(s, payload, r, vals, idxs) for (s, vals, idxs), (_, payload) in zip(carry, problems))

    t = problems[0][0].shape[1]
    init = tuple((s, jnp.zeros((k, t), F32), jnp.zeros((k, t), jnp.int32)) for s, _ in problems)
    return [(vals, idxs) for _, vals, idxs in lax.fori_loop(0, k, body, init)]


def _pair_candidates(s1, i1, s2, i2):
    neg = jnp.float32(-jnp.inf)
    sub = lax.broadcasted_iota(jnp.int32, (SUBLANES, s1.shape[1]), 0)
    cs, ci = [s1[0:1] + s2], [i1[0:1] * N_KEYS + i2]
    for a in range(1, PEER_TOPK // 2):
        keep = sub < PEER_TOPK // (a + 1)
        cs.append(jnp.where(keep, s1[a:a + 1] + s2[:SUBLANES], neg))
        ci.append(i1[a:a + 1] * N_KEYS + i2[:SUBLANES])
    cs.append(s1[PEER_TOPK // 2:] + s2[0:1])
    ci.append(i1[PEER_TOPK // 2:] * N_KEYS + i2[0:1])
    return jnp.concatenate(cs, axis=0), jnp.concatenate(ci, axis=0)


def _peer_front_kernel(x_ref, g_ref, wq_hi_ref, wq_lo_ref, keys_hi_ref, keys_lo_ref, hn_ref, idx_ref, gate_ref, *,
                       lane_chunk):
    hn = _rms(x_ref[...], g_ref[...])

    @pl.when(pl.program_id(1) == 0)
    def _():
        hn_ref[...] = hn

    q = _dot3(_split(hn), (wq_hi_ref[...], wq_lo_ref[...]), ((1,), (0,)))
    cands = []
    for c in range(hn.shape[0] // lane_chunk):
        tok = slice(c * lane_chunk, (c + 1) * lane_chunk)
        scores = [_dot3((keys_hi_ref[0, half], keys_lo_ref[0, half]),
                        _split(q[tok, half * PEER_DHALF:(half + 1) * PEER_DHALF]), ((1,), (1,))) for half in range(2)]
        (s1, i1), (s2, i2) = _top_rows([(sc, None) for sc in scores], PEER_TOPK)
        cands.append(_pair_candidates(s1, i1, s2, i2))
    for c, (best_s, best_i) in enumerate(_top_rows(cands, PEER_TOPK)):
        tok = slice(c * lane_chunk, (c + 1) * lane_chunk)
        e = jnp.exp(best_s - best_s[0:1, :])
        gate_ref[0, :, tok] = e / jnp.sum(e, axis=0, keepdims=True)
        idx_ref[0, :, tok] = best_i


def _peer_front(x2, g, w_q, keys):
    n, d = x2.shape
    t = ROW_BLOCK
    nb = n // t
    dq = 2 * PEER_DHALF
    k = PEER_HEADS * PEER_TOPK
    return pl.pallas_call(
        functools.partial(_peer_front_kernel, lane_chunk=128),
        grid=(nb, PEER_HEADS),
        in_specs=[
            pl.BlockSpec((t, d), lambda i, h: (i, 0)),
            _full((1, d)),
            pl.BlockSpec((d, dq), lambda i, h: (0, h)),
            pl.BlockSpec((d, dq), lambda i, h: (0, h)),
            pl.BlockSpec((1, 2, N_KEYS, PEER_DHALF), lambda i, h: (h, 0, 0, 0)),
            pl.BlockSpec((1, 2, N_KEYS, PEER_DHALF), lambda i, h: (h, 0, 0, 0)),
        ],
        out_specs=[
            pl.BlockSpec((t, d), lambda i, h: (i, 0)),
            pl.BlockSpec((1, PEER_TOPK, t), lambda i, h: (i, h, 0)),
            pl.BlockSpec((1, PEER_TOPK, t), lambda i, h: (i, h, 0)),
        ],
        out_shape=[
            jax.ShapeDtypeStruct((n, d), F32),
            jax.ShapeDtypeStruct((nb, k, t), jnp.int32),
            jax.ShapeDtypeStruct((nb, k, t), F32),
        ],
        compiler_params=_params(("parallel", "arbitrary")),
        name="peer_front",
    )(x2, g.reshape(1, d), *_split(w_q), *_split(keys))


def _row_sums(p):
    k = p.shape[0]
    p = p.reshape(k // SUBLANES, SUBLANES, SUBLANES, LANES)
    s = lax.broadcasted_iota(jnp.int32, (1, 1, SUBLANES, LANES), 2)
    a, b = p[:, :4], p[:, 4:]
    m = (s & 4) == 0
    p = jnp.where(m, a, b) + pltpu.roll(jnp.where(m, b, a), 4, axis=2)
    for dist in (2, 1):
        h = p.shape[1] // 2
        a, b = p[:, :h], p[:, h:]
        m = (s & dist) == 0
        p = jnp.where(m, a + pltpu.roll(a, SUBLANES - dist, axis=2), b + pltpu.roll(b, dist, axis=2))
    return p.reshape(k, LANES)


def _peer_experts_kernel(idx_ref, idx_next_ref, hn_ref, gate_ref, x_ref, uv_hbm, o_ref, buf_a, buf_b, wbuf, sem, *, sub):
    step = pl.program_id(0)
    last_step = pl.num_programs(0) - 1
    t = hn_ref.shape[0]
    k = gate_ref.shape[1]
    n_sub = t // sub
    bufs = (buf_a, buf_b)
    lane = lax.broadcasted_iota(jnp.int32, (1, t), 1)
    hi_mask = jnp.int32(-65536)

    def issue_token(idx, base, tk, slot):
        for r in range(k):
            e = idx[base + tk * k + r]
            pltpu.make_async_copy(uv_hbm.at[e], bufs[slot].at[tk * k + r], sem.at[slot]).start(priority=r % 2)

    def wait(slot):
        pltpu.make_async_copy(uv_hbm.at[pl.ds(0, sub * k)], bufs[slot], sem.at[slot]).wait()

    def compute_token(sb, tk, slot):
        buf = bufs[slot]
        tok = sb * sub + tk
        u = lax.bitcast_convert_type(buf[pl.ds(tk * k, k)] << 16, F32)
        a = jnp.sum(_row_sums(u * hn_ref[tok][None]), axis=-1, keepdims=True)
        gcol = jnp.sum(jnp.where(lane == tok, gate_ref[0], 0.0), axis=-1, keepdims=True)
        wbuf[tk] = jnp.broadcast_to(gcol * jax.nn.gelu(a), (k, LANES))
        acc = [None] * 4
        for r in range(k):
            v = lax.bitcast_convert_type(buf[tk * k + r] & hi_mask, F32)
            term = jnp.broadcast_to(wbuf[tk, r:r + 1, :], (SUBLANES, LANES)) * v
            acc[r % 4] = term if acc[r % 4] is None else acc[r % 4] + term
        o_ref[tok] = x_ref[tok] + ((acc[0] + acc[1]) + (acc[2] + acc[3]))

    def run_sub(sb, slot, next_idx, next_base):
        wait(slot)

        def body(tk, carry):
            if next_idx is not None:
                issue_token(next_idx, next_base, tk, 1 - slot)
            compute_token(sb, tk, slot)
            return carry

        lax.fori_loop(0, sub, body, 0)

    @pl.when(step == 0)
    def _():
        lax.fori_loop(0, sub, lambda tk, c: (issue_token(idx_ref, 0, tk, 0), c)[1], 0)

    def pair(sp, carry):
        run_sub(2 * sp, 0, idx_ref, (2 * sp + 1) * sub * k)
        run_sub(2 * sp + 1, 1, idx_ref, (2 * sp + 2) * sub * k)
        return carry

    lax.fori_loop(0, n_sub // 2 - 1, pair, 0)
    run_sub(n_sub - 2, 0, idx_ref, (n_sub - 1) * sub * k)

    @pl.when(step < last_step)
    def _():
        run_sub(n_sub - 1, 1, idx_next_ref, 0)

    @pl.when(step == last_step)
    def _():
        run_sub(n_sub - 1, 1, None, 0)


def _peer_experts(x2, hn, idx_flat, gate_t, uv):
    n, d = x2.shape
    nb, k, t = gate_t.shape
    sub = GATHER_SUB
    assert t % (2 * sub) == 0 and k % SUBLANES == 0 and d == SUBLANES * LANES
    as_tiles = lambda a: a.reshape(-1, SUBLANES, LANES)
    row = pl.BlockSpec((t, SUBLANES, LANES), lambda i: (i, 0, 0))
    out = pl.pallas_call(
        functools.partial(_peer_experts_kernel, sub=sub),
        grid=(nb,),
        in_specs=[
            pl.BlockSpec((t * k,), lambda i: (i,), memory_space=pltpu.SMEM),
            pl.BlockSpec((t * k,), lambda i: (jnp.minimum(i + 1, nb - 1),), memory_space=pltpu.SMEM),
            row,
            pl.BlockSpec((1, k, t), lambda i: (i, 0, 0)),
            row,
            pl.BlockSpec(memory_space=pl.ANY),
        ],
        out_specs=row,
        out_shape=jax.ShapeDtypeStruct((n, SUBLANES, LANES), F32),
        scratch_shapes=[
            pltpu.VMEM((sub * k, SUBLANES, LANES), jnp.int32),
            pltpu.VMEM((sub * k, SUBLANES, LANES), jnp.int32),
            pltpu.VMEM((sub, k, LANES), F32),
            pltpu.SemaphoreType.DMA((2,)),
        ],
        compiler_params=_params(("arbitrary",)),
        name="peer_experts",
    )(idx_flat, idx_flat, as_tiles(hn), gate_t, as_tiles(x2), uv)
    return out.reshape(n, d)


def _pack_experts(u, v):
    bits = lambda a: lax.bitcast_convert_type(a.astype(jnp.bfloat16), jnp.uint16).astype(jnp.uint32)
    words = (bits(v) << 16) | bits(u)
    return lax.bitcast_convert_type(words, jnp.int32).reshape(-1, SUBLANES, LANES)


def _pack_halves(t):
    half = t.shape[1] // 2
    bits = lambda a: lax.bitcast_convert_type(a.astype(BF16), jnp.uint16).astype(jnp.uint32)
    return lax.bitcast_convert_type((bits(t[:, half:]) << 16) | bits(t[:, :half]), jnp.int32)


def _sc_peer_experts(idx, hn, gate, x, u16, v16):
    n, k = idx.shape
    d = hn.shape[1]
    half = d // 2
    lanes_n = SC_LANES
    workers = SC_CORES * SC_SUBCORES
    assert n % (workers * SC_TOKENS) == 0 and k % SC_U_ROWS == 0 and half % (lanes_n * SC_ACC_WORDS) == 0
    per_w = n // workers
    mesh = plsc.VectorSubcoreMesh(core_axis_name="c", subcore_axis_name="s")
    hi_mask = jnp.int32(-65536)

    def words(w):
        return lax.bitcast_convert_type(w << 16, F32), lax.bitcast_convert_type(w & hi_mask, F32)

    @functools.partial(
        pl.kernel, mesh=mesh, out_type=jax.ShapeDtypeStruct((n, d), F32),
        scratch_types=[
            pltpu.VMEM((SC_TOKENS, k), jnp.int32),
            pltpu.VMEM((SC_TOKENS, d), F32),
            pltpu.VMEM((SC_TOKENS, k), F32),
            pltpu.VMEM((SC_TOKENS, d), F32),
            pltpu.VMEM((SC_U_ROWS, half), jnp.int32),
            pltpu.VMEM((k, half), jnp.int32),
            pltpu.VMEM((k, lanes_n), F32),
            pltpu.VMEM((k,), F32),
            pltpu.SemaphoreType.DMA,
            pltpu.SemaphoreType.DMA,
        ],
        compiler_params=pltpu.CompilerParams(needs_layout_passes=False),
        name="peer_experts_sc",
    )
    def kern(idx_hbm, hn_hbm, gate_hbm, x_hbm, u_hbm, v_hbm, o_hbm, idx_v, hn_v, gate_v, x_v, urows, vrows, part, w_v,
             sem_u, sem_v):
        wid = lax.axis_index("s") * SC_CORES + lax.axis_index("c")
        lanes = lax.iota(jnp.int32, lanes_n)

        @pl.loop(0, per_w // SC_TOKENS)
        def _(bi):
            base = wid * per_w + bi * SC_TOKENS
            pltpu.sync_copy(idx_hbm.at[pl.ds(base, SC_TOKENS)], idx_v)
            pltpu.sync_copy(hn_hbm.at[pl.ds(base, SC_TOKENS)], hn_v)
            pltpu.sync_copy(gate_hbm.at[pl.ds(base, SC_TOKENS)], gate_v)
            pltpu.sync_copy(x_hbm.at[pl.ds(base, SC_TOKENS)], x_v)

            @pl.loop(0, SC_TOKENS)
            def _(ti):
                vcopy = pltpu.async_copy(v_hbm.at[idx_v.at[ti]], vrows, sem_v)

                @pl.loop(0, k // SC_U_ROWS)
                def _(c):
                    pltpu.async_copy(u_hbm.at[idx_v.at[ti, pl.ds(c * SC_U_ROWS, SC_U_ROWS)]], urows, sem_u).wait()

                    @pl.loop(0, SC_U_ROWS)
                    def _(r):
                        acc = jnp.zeros((lanes_n,), F32)
                        for j in range(half // lanes_n):
                            lo, hi = words(urows[r, pl.ds(j * lanes_n, lanes_n)])
                            acc = (acc + lo * hn_v[ti, pl.ds(j * lanes_n, lanes_n)]
                                   + hi * hn_v[ti, pl.ds(half + j * lanes_n, lanes_n)])
                        part[c * SC_U_ROWS + r, :] = acc

                @pl.loop(0, k // lanes_n)
                def _(g):
                    rows = g * lanes_n + lanes
                    a = jnp.zeros((lanes_n,), F32)
                    for l in range(lanes_n):
                        a = a + plsc.load_gather(part, [rows, jnp.full((lanes_n,), l, jnp.int32)])
                    z = math.sqrt(2.0 / math.pi) * (a + 0.044715 * a * a * a)
                    tanh = 1.0 - 2.0 / (jnp.exp(2.0 * z) + 1.0)
                    w_v[pl.ds(g * lanes_n, lanes_n)] = gate_v[ti, pl.ds(g * lanes_n, lanes_n)] * (0.5 * a * (1.0 + tanh))

                vcopy.wait()

                for jb in range(half // lanes_n // SC_ACC_WORDS):
                    def body(r, accs, jb=jb):
                        ws = plsc.load_gather(w_v, [jnp.zeros((lanes_n,), jnp.int32) + r])
                        out = []
                        for jj in range(SC_ACC_WORDS):
                            lo, hi = words(vrows[r, pl.ds((jb * SC_ACC_WORDS + jj) * lanes_n, lanes_n)])
                            out += [accs[2 * jj] + ws * lo, accs[2 * jj + 1] + ws * hi]
                        return tuple(out)

                    accs = lax.fori_loop(0, k, body, tuple(jnp.zeros((lanes_n,), F32) for _ in range(2 * SC_ACC_WORDS)))
                    for jj in range(SC_ACC_WORDS):
                        col = (jb * SC_ACC_WORDS + jj) * lanes_n
                        x_v[ti, pl.ds(col, lanes_n)] = x_v[ti, pl.ds(col, lanes_n)] + accs[2 * jj]
                        x_v[ti, pl.ds(half + col, lanes_n)] = x_v[ti, pl.ds(half + col, lanes_n)] + accs[2 * jj + 1]

            pltpu.sync_copy(x_v, o_hbm.at[pl.ds(base, SC_TOKENS)])

    return kern(idx, hn, gate, x, u16, v16)


def _peer_layer(x2, g, w_q, keys, u, v):
    hn, idx_t, gate_t = _peer_front(x2, g, w_q, keys)
    nb, k, t = idx_t.shape
    n = nb * t
    idx_rows = idx_t.transpose(0, 2, 1).reshape(n, k)
    nb_sc = (nb * SC_SHARE_NUM) // SC_SHARE_DEN
    n_tc = (nb - nb_sc) * t
    out_tc = _peer_experts(x2[:n_tc], hn[:n_tc], idx_rows[:n_tc].reshape(n_tc * k), gate_t[:nb - nb_sc],
                           _pack_experts(u, v))
    if nb_sc == 0:
        return out_tc
    gate_rows = gate_t[nb - nb_sc:].transpose(0, 2, 1).reshape(n - n_tc, k)
    out_sc = _sc_peer_experts(idx_rows[n_tc:], hn[n_tc:], gate_rows, x2[n_tc:], _pack_halves(u), _pack_halves(v))
    return jnp.concatenate([out_tc, out_sc], axis=0)


def _ple_kernel(x_ref, p_ref, g_ref, wg_ref, wp_ref, fg_ref, o_ref, *, final):
    x = x_ref[...]
    gate = jax.nn.sigmoid(_dot(_rms(x, g_ref[...]), wg_ref[...]))
    y = x + gate * _dot(p_ref[...], wp_ref[...])
    o_ref[...] = _rms(y, fg_ref[...]) if final else y


def _ple_layer(x2, p2, g, w_g, w_p, final_g, final):
    n, d = x2.shape
    t = ROW_BLOCK
    row = lambda width: pl.BlockSpec((t, width), lambda i: (i, 0))
    return pl.pallas_call(
        functools.partial(_ple_kernel, final=final),
        grid=(n // t,),
        in_specs=[row(d), row(p2.shape[1]), _full((1, d)), _full(w_g.shape), _full(w_p.shape), _full((1, d))],
        out_specs=row(d),
        out_shape=jax.ShapeDtypeStruct((n, d), F32),
        compiler_params=_params(("parallel",)),
        name="ple",
    )(x2, p2, g.reshape(1, d), w_g.astype(BF16), w_p.astype(BF16), final_g.reshape(1, d))


def kernel(x, p, positions, mix_norm, ffn_norm, pool_w, pool_scale, kv_in_norm, w_dkv, kv_norm, w_ukv, w_dq, q_norm, w_uq, w_o, peer_wq, peer_keys, peer_u, peer_v, ple_norm, ple_wg, ple_wp, final_norm):
    b, s, d = x.shape
    depth = p.shape[0]
    n_a = pool_w.shape[0]
    x2 = x.reshape(b * s, d)
    cos, sin = _rope_tables(positions)
    k_sh = v_sh = None
    for i in range(depth):
        if i < n_a:
            x2 = _pool_layer(x2, s, mix_norm[i], pool_w[i], pool_scale[i])
        else:
            if i == n_a:
                k_sh, v_sh = _shared_kv(x2, cos, sin, kv_in_norm, w_dkv, kv_norm, w_ukv)
            j = i - n_a
            q = _q_proj(x2, cos, sin, mix_norm[i], w_dq[j], q_norm[j], w_uq[j])
            o = _flash_attention(q, k_sh, v_sh, b, s)
            x2 = _out_proj(x2, o, w_o[j])
        x2 = _peer_layer(x2, ffn_norm[i], peer_wq[i], peer_keys[i], peer_u[i], peer_v[i])
        x2 = _ple_layer(x2, p[i].reshape(b * s, -1), ple_norm[i], ple_wg[i], ple_wp[i], final_norm, i == depth - 1)
    return x2.reshape(b, s, d)
```

```python
import functools
import math

import jax
import jax.numpy as jnp
from jax import lax
from jax.experimental import pallas as pl
from jax.experimental.pallas import tpu as pltpu
from jax.experimental.pallas import tpu_sc as plsc

F32 = jnp.float32
BF16 = jnp.bfloat16
EPS = 1e-6

POOL_WINDOWS = (2, 4, 8, 16)
POOL_HALO = 16
MLA_HEADS = 8
QK_NOPE = 128
QK_ROPE = 64
V_HEAD = 128
KV_RANK = 256
ROPE_THETA = 10000.0
ATTN_SCALE = 1.0 / math.sqrt(QK_NOPE + QK_ROPE)
PEER_HEADS = 8
N_KEYS = 128
PEER_TOPK = 16
PEER_DHALF = 128

SUBLANES = 8
LANES = 128
ROW_BLOCK = 256
POOL_BLOCK = 512
ATTN_BLOCK = 512
ATTN_DTYPE = jnp.bfloat16
GATHER_SUB = 8
VMEM_LIMIT = 56 * 1024 * 1024
SC_CORES = 2
SC_SUBCORES = 16
SC_LANES = 16
SC_TOKENS = 16
SC_CHUNK = 32
SC_SLOTS = 4
SC_AHEAD = 3
SC_ROW_GROUP = 4
SC_ACC_WORDS = 8
SC_SHARE_NUM, SC_SHARE_DEN = 33, 64


def _params(sem):
    return pltpu.CompilerParams(dimension_semantics=sem, vmem_limit_bytes=VMEM_LIMIT)


def _rms(x, g):
    ms = jnp.mean(x * x, axis=-1, keepdims=True)
    return x * lax.rsqrt(ms + EPS) * g


def _dot(a, w):
    return jnp.dot(a.astype(w.dtype), w, preferred_element_type=F32)


def _split(a):
    hi = a.astype(BF16)
    return hi, (a - hi.astype(F32)).astype(BF16)


def _dot3(a, b, dims):
    mm = lambda x, y: lax.dot_general(x, y, (dims, ((), ())), preferred_element_type=F32)
    return mm(a[0], b[0]) + (mm(a[0], b[1]) + mm(a[1], b[0]))


def _full(shape):
    return pl.BlockSpec(shape, lambda *_: (0,) * len(shape))


def _pool_kernel(x_ref, halo_ref, g_ref, w_ref, sc_ref, o_ref, *, ts, blocks_per_seq):
    blk = pl.program_id(0) % blocks_per_seq
    x = x_ref[...]
    g = g_ref[...]
    h = _rms(x, g)
    hh = jnp.where(blk == 0, 0.0, _rms(halo_ref[...], g))
    full = jnp.concatenate([hh, h], axis=0)
    t = blk * ts + lax.broadcasted_iota(jnp.int32, (ts, 1), 0)
    gw = w_ref.shape[-1]
    outs = []
    for gi, w in enumerate(POOL_WINDOWS):
        p = full[:, gi * gw:(gi + 1) * gw]
        step = 1
        while step < w:
            p = p + pltpu.roll(p, step, axis=0)
            step *= 2
        win = p[POOL_HALO:, :]
        cnt = jnp.minimum(t + 1, w).astype(F32)
        pooled = win / cnt - h[:, gi * gw:(gi + 1) * gw]
        outs.append(_dot(pooled, w_ref[gi]))
    o_ref[...] = x + jnp.concatenate(outs, axis=-1) * sc_ref[...]


def _pool_layer(x2, seq, g, w, scale):
    n, d = x2.shape
    ts = min(POOL_BLOCK, seq)
    assert seq % ts == 0 and ts % POOL_HALO == 0 and all(wd & (wd - 1) == 0 and wd <= POOL_HALO for wd in POOL_WINDOWS)
    hb = ts // POOL_HALO
    return pl.pallas_call(
        functools.partial(_pool_kernel, ts=ts, blocks_per_seq=seq // ts),
        grid=(n // ts,),
        in_specs=[
            pl.BlockSpec((ts, d), lambda i: (i, 0)),
            pl.BlockSpec((POOL_HALO, d), lambda i: (jnp.maximum(i * hb - 1, 0), 0)),
            _full((1, d)),
            _full(w.shape),
            _full((1, d)),
        ],
        out_specs=pl.BlockSpec((ts, d), lambda i: (i, 0)),
        out_shape=jax.ShapeDtypeStruct((n, d), F32),
        compiler_params=_params(("parallel",)),
        name="pool_mixer",
    )(x2, x2, g.reshape(1, d), w.astype(BF16), scale.reshape(1, d))


def _rope_kernel(pos_ref, inv_ref, cos_ref, sin_ref):
    ang = inv_ref[...] * pos_ref[0].astype(F32)
    cos_ref[0] = jnp.cos(ang)
    sin_ref[0] = jnp.sin(ang)


def _rope_tables(positions):
    b, s = positions.shape
    half = QK_ROPE // 2
    inv = ROPE_THETA ** (-jnp.arange(0, QK_ROPE, 2, dtype=F32) / QK_ROPE)
    cos_t, sin_t = pl.pallas_call(
        _rope_kernel,
        grid=(b,),
        in_specs=[pl.BlockSpec((1, 1, s), lambda i: (i, 0, 0)), _full((half, 1))],
        out_specs=[pl.BlockSpec((1, half, s), lambda i: (i, 0, 0))] * 2,
        out_shape=[jax.ShapeDtypeStruct((b, half, s), F32)] * 2,
        compiler_params=_params(("parallel",)),
        name="rope_tables",
    )(positions.reshape(b, 1, s), inv.reshape(half, 1))
    to_rows = lambda a: a.transpose(0, 2, 1).reshape(b * s, half)
    return to_rows(cos_t), to_rows(sin_t)


def _kv_kernel(x_ref, cos_ref, sin_ref, gin_ref, wdkv_ref, gkv_ref, wukv_ref, k_ref, v_ref):
    h = _rms(x_ref[...], gin_ref[...])
    ckv = _dot(h, wdkv_ref[...])
    c = _rms(ckv[:, :KV_RANK], gkv_ref[...])
    half = QK_ROPE // 2
    x1 = ckv[:, KV_RANK:KV_RANK + half]
    x2 = ckv[:, KV_RANK + half:]
    cos = cos_ref[...]
    sin = sin_ref[...]
    k1 = x1 * cos - x2 * sin
    k2 = x2 * cos + x1 * sin
    kv = _dot(c, wukv_ref[...])
    for hd in range(MLA_HEADS):
        k_ref[hd] = jnp.concatenate([kv[:, hd * QK_NOPE:(hd + 1) * QK_NOPE], k1, k2], axis=-1).astype(k_ref.dtype)
        off = MLA_HEADS * QK_NOPE + hd * V_HEAD
        v_ref[hd] = kv[:, off:off + V_HEAD].astype(v_ref.dtype)


def _split_heads(w, bounds):
    w3 = w.reshape(w.shape[0], MLA_HEADS, -1)
    return jnp.concatenate([w3[:, :, lo:hi].reshape(w.shape[0], -1) for lo, hi in bounds], axis=1)


def _shared_kv(x2, cos, sin, kv_in_norm, w_dkv, kv_norm, w_ukv):
    n, d = x2.shape
    t = ROW_BLOCK
    half = QK_ROPE // 2
    w_ukv_p = _split_heads(w_ukv, [(0, QK_NOPE), (QK_NOPE, QK_NOPE + V_HEAD)])
    row = lambda width: pl.BlockSpec((t, width), lambda i: (i, 0))
    return pl.pallas_call(
        _kv_kernel,
        grid=(n // t,),
        in_specs=[row(d), row(half), row(half), _full((1, d)), _full(w_dkv.shape), _full((1, KV_RANK)),
                  _full(w_ukv_p.shape)],
        out_specs=[pl.BlockSpec((MLA_HEADS, t, QK_NOPE + QK_ROPE), lambda i: (0, i, 0)),
                   pl.BlockSpec((MLA_HEADS, t, V_HEAD), lambda i: (0, i, 0))],
        out_shape=[jax.ShapeDtypeStruct((MLA_HEADS, n, QK_NOPE + QK_ROPE), ATTN_DTYPE),
                   jax.ShapeDtypeStruct((MLA_HEADS, n, V_HEAD), ATTN_DTYPE)],
        compiler_params=_params(("parallel",)),
        name="shared_kv",
    )(x2, cos, sin, kv_in_norm.reshape(1, d), w_dkv.astype(BF16), kv_norm.reshape(1, KV_RANK), w_ukv_p.astype(BF16))


def _q_kernel(x_ref, cos_ref, sin_ref, g_ref, wdq_ref, gq_ref, wuq_ref, q_ref):
    h = _rms(x_ref[...], g_ref[...])
    cq = _rms(_dot(h, wdq_ref[...]), gq_ref[...])
    q = _dot(cq, wuq_ref[...]) * ATTN_SCALE
    half = QK_ROPE // 2
    nope = MLA_HEADS * QK_NOPE
    cos = jnp.concatenate([cos_ref[...]] * MLA_HEADS, axis=-1)
    sin = jnp.concatenate([sin_ref[...]] * MLA_HEADS, axis=-1)
    p1 = q[:, nope:nope + MLA_HEADS * half]
    p2 = q[:, nope + MLA_HEADS * half:]
    r1 = p1 * cos - p2 * sin
    r2 = p2 * cos + p1 * sin
    for hd in range(MLA_HEADS):
        q_ref[hd] = jnp.concatenate([q[:, hd * QK_NOPE:(hd + 1) * QK_NOPE], r1[:, hd * half:(hd + 1) * half],
                                     r2[:, hd * half:(hd + 1) * half]], axis=-1).astype(q_ref.dtype)


def _q_proj(x2, cos, sin, g, w_dq, q_norm, w_uq):
    n, d = x2.shape
    t = ROW_BLOCK
    half = QK_ROPE // 2
    rank = w_dq.shape[1]
    w_uq_p = _split_heads(w_uq, [(0, QK_NOPE), (QK_NOPE, QK_NOPE + half), (QK_NOPE + half, QK_NOPE + QK_ROPE)])
    row = lambda width: pl.BlockSpec((t, width), lambda i: (i, 0))
    return pl.pallas_call(
        _q_kernel,
        grid=(n // t,),
        in_specs=[row(d), row(half), row(half), _full((1, d)), _full(w_dq.shape), _full((1, rank)),
                  _full(w_uq_p.shape)],
        out_specs=pl.BlockSpec((MLA_HEADS, t, QK_NOPE + QK_ROPE), lambda i: (0, i, 0)),
        out_shape=jax.ShapeDtypeStruct((MLA_HEADS, n, QK_NOPE + QK_ROPE), ATTN_DTYPE),
        compiler_params=_params(("parallel",)),
        name="mla_q",
    )(x2, cos, sin, g.reshape(1, d), w_dq.astype(BF16), q_norm.reshape(1, rank), w_uq_p.astype(BF16))


def _flash_kernel(q_ref, k_ref, v_ref, o_ref, m_sc, l_sc, acc_sc, *, tb):
    qi = pl.program_id(2)
    ki = pl.program_id(3)

    @pl.when(ki == 0)
    def _():
        m_sc[...] = jnp.full_like(m_sc, -jnp.inf)
        l_sc[...] = jnp.zeros_like(l_sc)
        acc_sc[...] = jnp.zeros_like(acc_sc)

    def update(masked):
        s = lax.dot_general(q_ref[0], k_ref[0], (((1,), (1,)), ((), ())), preferred_element_type=F32)
        if masked:
            row = lax.broadcasted_iota(jnp.int32, (tb, tb), 0)
            col = lax.broadcasted_iota(jnp.int32, (tb, tb), 1)
            s = jnp.where(col <= row, s, -jnp.inf)
        m_prev = m_sc[...]
        m_new = jnp.maximum(m_prev, jnp.max(s, axis=-1, keepdims=True))
        p = jnp.exp(s - m_new)
        alpha = jnp.exp(m_prev - m_new)
        l_sc[...] = alpha * l_sc[...] + jnp.sum(p, axis=-1, keepdims=True)
        acc_sc[...] = alpha * acc_sc[...] + jnp.dot(p.astype(v_ref.dtype), v_ref[0], preferred_element_type=F32)
        m_sc[...] = m_new

    pl.when(ki < qi)(functools.partial(update, False))
    pl.when(ki == qi)(functools.partial(update, True))

    @pl.when(ki == qi)
    def _():
        o_ref[...] = (acc_sc[...] / l_sc[...]).astype(o_ref.dtype)


def _flash_attention(q, k, v, batch, seq):
    tb = min(ATTN_BLOCK, seq)
    nb = seq // tb
    n = batch * seq
    dk = q.shape[-1]
    return pl.pallas_call(
        functools.partial(_flash_kernel, tb=tb),
        grid=(batch, MLA_HEADS, nb, nb),
        in_specs=[
            pl.BlockSpec((1, tb, dk), lambda b, h, qi, ki: (h, b * nb + qi, 0)),
            pl.BlockSpec((1, tb, dk), lambda b, h, qi, ki: (h, b * nb + jnp.minimum(ki, qi), 0)),
            pl.BlockSpec((1, tb, V_HEAD), lambda b, h, qi, ki: (h, b * nb + jnp.minimum(ki, qi), 0)),
        ],
        out_specs=pl.BlockSpec((tb, V_HEAD), lambda b, h, qi, ki: (b * nb + qi, h)),
        out_shape=jax.ShapeDtypeStruct((n, MLA_HEADS * V_HEAD), ATTN_DTYPE),
        scratch_shapes=[pltpu.VMEM((tb, 1), F32), pltpu.VMEM((tb, 1), F32), pltpu.VMEM((tb, V_HEAD), F32)],
        compiler_params=_params(("parallel", "parallel", "parallel", "arbitrary")),
        name="mla_flash",
    )(q, k, v)


def _oproj_kernel(x_ref, o_ref, w_ref, y_ref):
    y_ref[...] = x_ref[...] + _dot(o_ref[...], w_ref[...])


def _out_proj(x2, o, w_o):
    n, d = x2.shape
    t = ROW_BLOCK
    row = lambda width: pl.BlockSpec((t, width), lambda i: (i, 0))
    return pl.pallas_call(
        _oproj_kernel,
        grid=(n // t,),
        in_specs=[row(d), row(o.shape[1]), _full(w_o.shape)],
        out_specs=row(d),
        out_shape=jax.ShapeDtypeStruct((n, d), F32),
        compiler_params=_params(("parallel",)),
        name="mla_out",
    )(x2, o, w_o.astype(BF16))


def _top_rows(problems, k):
    def pick(s, payload, r, vals, idxs):
        rows, t = s.shape
        iota = lax.broadcasted_iota(jnp.int32, (rows, t), 0)
        out_row = lax.broadcasted_iota(jnp.int32, (k, t), 0)
        m = jnp.max(s, axis=0, keepdims=True)
        pos = jnp.min(jnp.where(s == m, iota, rows), axis=0, keepdims=True)
        hit = iota == pos
        got = pos if payload is None else jnp.sum(jnp.where(hit, payload, 0), axis=0, keepdims=True)
        return (jnp.where(hit, -jnp.inf, s), jnp.where(out_row == r, m, vals), jnp.where(out_row == r, got, idxs))

    def body(r, carry):
        return tuple(pick(s, payload, r, vals, idxs) for (s, vals, idxs), (_, payload) in zip(carry, problems))

    t = problems[0][0].shape[1]
    init = tuple((s, jnp.zeros((k, t), F32), jnp.zeros((k, t), jnp.int32)) for s, _ in problems)
    return [(vals, idxs) for _, vals, idxs in lax.fori_loop(0, k, body, init)]


def _pair_candidates(s1, i1, s2, i2):
    neg = jnp.float32(-jnp.inf)
    sub = lax.broadcasted_iota(jnp.int32, (SUBLANES, s1.shape[1]), 0)
    cs, ci = [s1[0:1] + s2], [i1[0:1] * N_KEYS + i2]
    for a in range(1, PEER_TOPK // 2):
        keep = sub < PEER_TOPK // (a + 1)
        cs.append(jnp.where(keep, s1[a:a + 1] + s2[:SUBLANES], neg))
        ci.append(i1[a:a + 1] * N_KEYS + i2[:SUBLANES])
    cs.append(s1[PEER_TOPK // 2:] + s2[0:1])
    ci.append(i1[PEER_TOPK // 2:] * N_KEYS + i2[0:1])
    return jnp.concatenate(cs, axis=0), jnp.concatenate(ci, axis=0)


def _peer_front_kernel(x_ref, g_ref, wq_hi_ref, wq_lo_ref, keys_hi_ref, keys_lo_ref, hn_ref, idx_ref, gate_ref, *,
                       lane_chunk):
    hn = _rms(x_ref[...], g_ref[...])

    @pl.when(pl.program_id(1) == 0)
    def _():
        hn_ref[...] = hn

    q = _dot3(_split(hn), (wq_hi_ref[...], wq_lo_ref[...]), ((1,), (0,)))
    cands = []
    for c in range(hn.shape[0] // lane_chunk):
        tok = slice(c * lane_chunk, (c + 1) * lane_chunk)
        scores = [_dot3((keys_hi_ref[0, half], keys_lo_ref[0, half]),
                        _split(q[tok, half * PEER_DHALF:(half + 1) * PEER_DHALF]), ((1,), (1,))) for half in range(2)]
        (s1, i1), (s2, i2) = _top_rows([(sc, None) for sc in scores], PEER_TOPK)
        cands.append(_pair_candidates(s1, i1, s2, i2))
    for c, (best_s, best_i) in enumerate(_top_rows(cands, PEER_TOPK)):
        tok = slice(c * lane_chunk, (c + 1) * lane_chunk)
        e = jnp.exp(best_s - best_s[0:1, :])
        gate_ref[0, :, tok] = e / jnp.sum(e, axis=0, keepdims=True)
        idx_ref[0, :, tok] = best_i


def _peer_front(x2, g, w_q, keys):
    n, d = x2.shape
    t = ROW_BLOCK
    nb = n // t
    dq = 2 * PEER_DHALF
    k = PEER_HEADS * PEER_TOPK
    return pl.pallas_call(
        functools.partial(_peer_front_kernel, lane_chunk=128),
        grid=(nb, PEER_HEADS),
        in_specs=[
            pl.BlockSpec((t, d), lambda i, h: (i, 0)),
            _full((1, d)),
            pl.BlockSpec((d, dq), lambda i, h: (0, h)),
            pl.BlockSpec((d, dq), lambda i, h: (0, h)),
            pl.BlockSpec((1, 2, N_KEYS, PEER_DHALF), lambda i, h: (h, 0, 0, 0)),
            pl.BlockSpec((1, 2, N_KEYS, PEER_DHALF), lambda i, h: (h, 0, 0, 0)),
        ],
        out_specs=[
            pl.BlockSpec((t, d), lambda i, h: (i, 0)),
            pl.BlockSpec((1, PEER_TOPK, t), lambda i, h: (i, h, 0)),
            pl.BlockSpec((1, PEER_TOPK, t), lambda i, h: (i, h, 0)),
        ],
        out_shape=[
            jax.ShapeDtypeStruct((n, d), F32),
            jax.ShapeDtypeStruct((nb, k, t), jnp.int32),
            jax.ShapeDtypeStruct((nb, k, t), F32),
        ],
        compiler_params=_params(("parallel", "arbitrary")),
        name="peer_front",
    )(x2, g.reshape(1, d), *_split(w_q), *_split(keys))


def _row_sums(p):
    k = p.shape[0]
    p = p.reshape(k // SUBLANES, SUBLANES, SUBLANES, LANES)
    s = lax.broadcasted_iota(jnp.int32, (1, 1, SUBLANES, LANES), 2)
    a, b = p[:, :4], p[:, 4:]
    m = (s & 4) == 0
    p = jnp.where(m, a, b) + pltpu.roll(jnp.where(m, b, a), 4, axis=2)
    for dist in (2, 1):
        h = p.shape[1] // 2
        a, b = p[:, :h], p[:, h:]
        m = (s & dist) == 0
        p = jnp.where(m, a + pltpu.roll(a, SUBLANES - dist, axis=2), b + pltpu.roll(b, dist, axis=2))
    return p.reshape(k, LANES)


def _peer_experts_kernel(idx_ref, idx_next_ref, hn_ref, gate_ref, x_ref, uv_hbm, o_ref, buf_a, buf_b, wbuf, sem, *, sub):
    step = pl.program_id(0)
    last_step = pl.num_programs(0) - 1
    t = hn_ref.shape[0]
    k = gate_ref.shape[1]
    n_sub = t // sub
    bufs = (buf_a, buf_b)
    lane = lax.broadcasted_iota(jnp.int32, (1, t), 1)
    hi_mask = jnp.int32(-65536)

    def issue_token(idx, base, tk, slot):
        for r in range(k):
            e = idx[base + tk * k + r]
            pltpu.make_async_copy(uv_hbm.at[e], bufs[slot].at[tk * k + r], sem.at[slot]).start(priority=r % 2)

    def wait(slot):
        pltpu.make_async_copy(uv_hbm.at[pl.ds(0, sub * k)], bufs[slot], sem.at[slot]).wait()

    def compute_token(sb, tk, slot):
        buf = bufs[slot]
        tok = sb * sub + tk
        u = lax.bitcast_convert_type(buf[pl.ds(tk * k, k)] << 16, F32)
        a = jnp.sum(_row_sums(u * hn_ref[tok][None]), axis=-1, keepdims=True)
        gcol = jnp.sum(jnp.where(lane == tok, gate_ref[0], 0.0), axis=-1, keepdims=True)
        wbuf[tk] = jnp.broadcast_to(gcol * jax.nn.gelu(a), (k, LANES))
        acc = [None] * 4
        for r in range(k):
            v = lax.bitcast_convert_type(buf[tk * k + r] & hi_mask, F32)
            term = jnp.broadcast_to(wbuf[tk, r:r + 1, :], (SUBLANES, LANES)) * v
            acc[r % 4] = term if acc[r % 4] is None else acc[r % 4] + term
        o_ref[tok] = x_ref[tok] + ((acc[0] + acc[1]) + (acc[2] + acc[3]))

    def run_sub(sb, slot, next_idx, next_base):
        wait(slot)

        def body(tk, carry):
            if next_idx is not None:
                issue_token(next_idx, next_base, tk, 1 - slot)
            compute_token(sb, tk, slot)
            return carry

        lax.fori_loop(0, sub, body, 0)

    @pl.when(step == 0)
    def _():
        lax.fori_loop(0, sub, lambda tk, c: (issue_token(idx_ref, 0, tk, 0), c)[1], 0)

    def pair(sp, carry):
        run_sub(2 * sp, 0, idx_ref, (2 * sp + 1) * sub * k)
        run_sub(2 * sp + 1, 1, idx_ref, (2 * sp + 2) * sub * k)
        return carry

    lax.fori_loop(0, n_sub // 2 - 1, pair, 0)
    run_sub(n_sub - 2, 0, idx_ref, (n_sub - 1) * sub * k)

    @pl.when(step < last_step)
    def _():
        run_sub(n_sub - 1, 1, idx_next_ref, 0)

    @pl.when(step == last_step)
    def _():
        run_sub(n_sub - 1, 1, None, 0)


def _peer_experts(x2, hn, idx_flat, gate_t, uv):
    n, d = x2.shape
    nb, k, t = gate_t.shape
    sub = GATHER_SUB
    assert t % (2 * sub) == 0 and k % SUBLANES == 0 and d == SUBLANES * LANES
    as_tiles = lambda a: a.reshape(-1, SUBLANES, LANES)
    row = pl.BlockSpec((t, SUBLANES, LANES), lambda i: (i, 0, 0))
    out = pl.pallas_call(
        functools.partial(_peer_experts_kernel, sub=sub),
        grid=(nb,),
        in_specs=[
            pl.BlockSpec((t * k,), lambda i: (i,), memory_space=pltpu.SMEM),
            pl.BlockSpec((t * k,), lambda i: (jnp.minimum(i + 1, nb - 1),), memory_space=pltpu.SMEM),
            row,
            pl.BlockSpec((1, k, t), lambda i: (i, 0, 0)),
            row,
            pl.BlockSpec(memory_space=pl.ANY),
        ],
        out_specs=row,
        out_shape=jax.ShapeDtypeStruct((n, SUBLANES, LANES), F32),
        scratch_shapes=[
            pltpu.VMEM((sub * k, SUBLANES, LANES), jnp.int32),
            pltpu.VMEM((sub * k, SUBLANES, LANES), jnp.int32),
            pltpu.VMEM((sub, k, LANES), F32),
            pltpu.SemaphoreType.DMA((2,)),
        ],
        compiler_params=_params(("arbitrary",)),
        name="peer_experts",
    )(idx_flat, idx_flat, as_tiles(hn), gate_t, as_tiles(x2), uv)
    return out.reshape(n, d)


def _pack_experts(u, v):
    bits = lambda a: lax.bitcast_convert_type(a.astype(jnp.bfloat16), jnp.uint16).astype(jnp.uint32)
    words = (bits(v) << 16) | bits(u)
    return lax.bitcast_convert_type(words, jnp.int32).reshape(-1, SUBLANES, LANES)


def _pack_halves(t):
    half = t.shape[1] // 2
    bits = lambda a: lax.bitcast_convert_type(a.astype(BF16), jnp.uint16).astype(jnp.uint32)
    return lax.bitcast_convert_type((bits(t[:, half:]) << 16) | bits(t[:, :half]), jnp.int32)


def _sc_peer_experts(idx, hn, gate, x, u16, v16):
    n, k = idx.shape
    d = hn.shape[1]
    half = d // 2
    ln = SC_LANES
    workers = SC_CORES * SC_SUBCORES
    nch = k // SC_CHUNK
    ntask = 2 * nch
    assert n % (workers * SC_TOKENS) == 0 and half % (ln * SC_ACC_WORDS) == 0 and SC_CHUNK % SC_ROW_GROUP == 0
    assert ntask % SC_SLOTS == 0 and SC_AHEAD < SC_SLOTS
    per_w = n // workers
    mesh = plsc.VectorSubcoreMesh(core_axis_name="c", subcore_axis_name="s")
    hi_mask = jnp.int32(-65536)

    def words(w):
        return lax.bitcast_convert_type(w << 16, F32), lax.bitcast_convert_type(w & hi_mask, F32)

    @functools.partial(
        pl.kernel, mesh=mesh, out_type=jax.ShapeDtypeStruct((n, d), F32),
        scratch_types=[
            pltpu.VMEM((SC_TOKENS, k), jnp.int32),
            pltpu.VMEM((SC_TOKENS, d), F32),
            pltpu.VMEM((SC_TOKENS, k), F32),
            pltpu.VMEM((SC_TOKENS, d), F32),
            [pltpu.VMEM((SC_CHUNK, half), jnp.int32)] * SC_SLOTS,
            pltpu.VMEM((k, ln), F32),
            pltpu.VMEM((k,), F32),
            [pltpu.SemaphoreType.DMA] * SC_SLOTS,
        ],
        compiler_params=pltpu.CompilerParams(needs_layout_passes=False),
        name="peer_experts_sc",
    )
    def kern(idx_hbm, hn_hbm, gate_hbm, x_hbm, u_hbm, v_hbm, o_hbm, idx_v, hn_v, gate_v, x_v, ring, part, w_v, sems):
        wid = lax.axis_index("s") * SC_CORES + lax.axis_index("c")
        lanes = lax.iota(jnp.int32, ln)

        def chunk_copy(ti, task):
            table = u_hbm if task < nch else v_hbm
            rows = idx_v.at[ti, pl.ds((task % nch) * SC_CHUNK, SC_CHUNK)]
            return pltpu.make_async_copy(table.at[rows], ring[task % SC_SLOTS], sems[task % SC_SLOTS])

        def down_chunk(ti, c, buf):
            @pl.loop(0, SC_CHUNK // SC_ROW_GROUP)
            def _(rg):
                r0 = rg * SC_ROW_GROUP
                lo_acc = [jnp.zeros((ln,), F32) for _ in range(SC_ROW_GROUP)]
                hi_acc = [jnp.zeros((ln,), F32) for _ in range(SC_ROW_GROUP)]
                for j in range(half // ln):
                    xl = hn_v[ti, pl.ds(j * ln, ln)]
                    xh = hn_v[ti, pl.ds(half + j * ln, ln)]
                    for i in range(SC_ROW_GROUP):
                        lo, hi = words(buf[r0 + i, pl.ds(j * ln, ln)])
                        lo_acc[i] = lo_acc[i] + lo * xl
                        hi_acc[i] = hi_acc[i] + hi * xh
                for i in range(SC_ROW_GROUP):
                    part[c * SC_CHUNK + r0 + i, :] = lo_acc[i] + hi_acc[i]

        def expert_weights(ti):
            @pl.loop(0, k // ln)
            def _(g):
                rows = g * ln + lanes
                a = jnp.zeros((ln,), F32)
                for l in range(ln):
                    a = a + plsc.load_gather(part, [rows, jnp.full((ln,), l, jnp.int32)])
                z = math.sqrt(2.0 / math.pi) * (a + 0.044715 * a * a * a)
                tanh = 1.0 - 2.0 / (jnp.exp(2.0 * z) + 1.0)
                w_v[pl.ds(g * ln, ln)] = gate_v[ti, pl.ds(g * ln, ln)] * (0.5 * a * (1.0 + tanh))

        def up_chunk(ti, c, buf):
            for jb in range(half // ln // SC_ACC_WORDS):
                cols = [(jb * SC_ACC_WORDS + jj) * ln for jj in range(SC_ACC_WORDS)]
                init = []
                for col in cols:
                    init += [x_v[ti, pl.ds(col, ln)], x_v[ti, pl.ds(half + col, ln)]]

                def body(r, accs, cols=cols):
                    ws = plsc.load_gather(w_v, [jnp.zeros((ln,), jnp.int32) + (c * SC_CHUNK + r)])
                    out = []
                    for jj, col in enumerate(cols):
                        lo, hi = words(buf[r, pl.ds(col, ln)])
                        out += [accs[2 * jj] + ws * lo, accs[2 * jj + 1] + ws * hi]
                    return tuple(out)

                accs = lax.fori_loop(0, SC_CHUNK, body, tuple(init))
                for jj, col in enumerate(cols):
                    x_v[ti, pl.ds(col, ln)] = accs[2 * jj]
                    x_v[ti, pl.ds(half + col, ln)] = accs[2 * jj + 1]

        @pl.loop(0, per_w // SC_TOKENS)
        def _(bi):
            base = wid * per_w + bi * SC_TOKENS
            pltpu.sync_copy(idx_hbm.at[pl.ds(base, SC_TOKENS)], idx_v)
            pltpu.sync_copy(hn_hbm.at[pl.ds(base, SC_TOKENS)], hn_v)
            pltpu.sync_copy(gate_hbm.at[pl.ds(base, SC_TOKENS)], gate_v)
            pltpu.sync_copy(x_hbm.at[pl.ds(base, SC_TOKENS)], x_v)
            for task in range(SC_AHEAD):
                chunk_copy(0, task).start()

            @pl.loop(0, SC_TOKENS)
            def _(ti):
                for task in range(ntask):
                    chunk_copy(ti, task).wait()
                    ahead = task + SC_AHEAD
                    if ahead < ntask:
                        chunk_copy(ti, ahead).start()
                    else:
                        @pl.when(ti + 1 < SC_TOKENS)
                        def _(ahead=ahead):
                            chunk_copy(ti + 1, ahead - ntask).start()
                    buf = ring[task % SC_SLOTS]
                    if task < nch:
                        down_chunk(ti, task, buf)
                        if task == nch - 1:
                            expert_weights(ti)
                    else:
                        up_chunk(ti, task - nch, buf)

            pltpu.sync_copy(x_v, o_hbm.at[pl.ds(base, SC_TOKENS)])

    return kern(idx, hn, gate, x, u16, v16)


def _peer_layer(x2, g, w_q, keys, u, v):
    hn, idx_t, gate_t = _peer_front(x2, g, w_q, keys)
    nb, k, t = idx_t.shape
    n = nb * t
    idx_rows = idx_t.transpose(0, 2, 1).reshape(n, k)
    nb_sc = (nb * SC_SHARE_NUM) // SC_SHARE_DEN
    n_tc = (nb - nb_sc) * t
    out_tc = _peer_experts(x2[:n_tc], hn[:n_tc], idx_rows[:n_tc].reshape(n_tc * k), gate_t[:nb - nb_sc],
                           _pack_experts(u, v))
    if nb_sc == 0:
        return out_tc
    gate_rows = gate_t[nb - nb_sc:].transpose(0, 2, 1).reshape(n - n_tc, k)
    out_sc = _sc_peer_experts(idx_rows[n_tc:], hn[n_tc:], gate_rows, x2[n_tc:], _pack_halves(u), _pack_halves(v))
    return jnp.concatenate([out_tc, out_sc], axis=0)


def _ple_kernel(x_ref, p_ref, g_ref, wg_ref, wp_ref, fg_ref, o_ref, *, final):
    x = x_ref[...]
    gate = jax.nn.sigmoid(_dot(_rms(x, g_ref[...]), wg_ref[...]))
    y = x + gate * _dot(p_ref[...], wp_ref[...])
    o_ref[...] = _rms(y, fg_ref[...]) if final else y


def _ple_layer(x2, p2, g, w_g, w_p, final_g, final):
    n, d = x2.shape
    t = ROW_BLOCK
    row = lambda width: pl.BlockSpec((t, width), lambda i: (i, 0))
    return pl.pallas_call(
        functools.partial(_ple_kernel, final=final),
        grid=(n // t,),
        in_specs=[row(d), row(p2.shape[1]), _full((1, d)), _full(w_g.shape), _full(w_p.shape), _full((1, d))],
        out_specs=row(d),
        out_shape=jax.ShapeDtypeStruct((n, d), F32),
        compiler_params=_params(("parallel",)),
        name="ple",
    )(x2, p2, g.reshape(1, d), w_g.astype(BF16), w_p.astype(BF16), final_g.reshape(1, d))


def kernel(x, p, positions, mix_norm, ffn_norm, pool_w, pool_scale, kv_in_norm, w_dkv, kv_norm, w_ukv, w_dq, q_norm, w_uq, w_o, peer_wq, peer_keys, peer_u, peer_v, ple_norm, ple_wg, ple_wp, final_norm):
    b, s, d = x.shape
    depth = p.shape[0]
    n_a = pool_w.shape[0]
    x2 = x.reshape(b * s, d)
    cos, sin = _rope_tables(positions)
    k_sh = v_sh = None
    for i in range(depth):
        if i < n_a:
            x2 = _pool_layer(x2, s, mix_norm[i], pool_w[i], pool_scale[i])
        else:
            if i == n_a:
                k_sh, v_sh = _shared_kv(x2, cos, sin, kv_in_norm, w_dkv, kv_norm, w_ukv)
            j = i - n_a
            q = _q_proj(x2, cos, sin, mix_norm[i], w_dq[j], q_norm[j], w_uq[j])
            o = _flash_attention(q, k_sh, v_sh, b, s)
            x2 = _out_proj(x2, o, w_o[j])
        x2 = _peer_layer(x2, ffn_norm[i], peer_wq[i], peer_keys[i], peer_u[i], peer_v[i])
        x2 = _ple_layer(x2, p[i].reshape(b * s, -1), ple_norm[i], ple_wg[i], ple_wp[i], final_norm, i == depth - 1)
    return x2.reshape(b, s, d)
```

```python
import functools
import math

import jax
import jax.numpy as jnp
from jax import lax
from jax.experimental import pallas as pl
from jax.experimental.pallas import tpu as pltpu
from jax.experimental.pallas import tpu_sc as plsc

F32 = jnp.float32
BF16 = jnp.bfloat16
EPS = 1e-6

POOL_WINDOWS = (2, 4, 8, 16)
POOL_HALO = 16
MLA_HEADS = 8
QK_NOPE = 128
QK_ROPE = 64
V_HEAD = 128
KV_RANK = 256
ROPE_THETA = 10000.0
ATTN_SCALE = 1.0 / math.sqrt(QK_NOPE + QK_ROPE)
PEER_HEADS = 8
N_KEYS = 128
PEER_TOPK = 16
PEER_DHALF = 128

SUBLANES = 8
LANES = 128
ROW_BLOCK = 256
POOL_BLOCK = 512
ATTN_BLOCK = 512
ATTN_DTYPE = jnp.bfloat16
GATHER_SUB = 8
VMEM_LIMIT = 56 * 1024 * 1024
SC_CORES = 2
SC_SUBCORES = 16
SC_LANES = 16
SC_TOKENS = 16
SC_CHUNK = 32
SC_SLOTS = 4
SC_AHEAD = 3
SC_ROW_GROUP = 4
SC_ACC_WORDS = 8
SC_SHARE_NUM, SC_SHARE_DEN = 50, 64
BATCH_CHAINS = 2


def _params(sem):
    return pltpu.CompilerParams(dimension_semantics=sem, vmem_limit_bytes=VMEM_LIMIT)


def _rms(x, g):
    ms = jnp.mean(x * x, axis=-1, keepdims=True)
    return x * lax.rsqrt(ms + EPS) * g


def _dot(a, w):
    return jnp.dot(a.astype(w.dtype), w, preferred_element_type=F32)


def _split(a):
    hi = a.astype(BF16)
    return hi, (a - hi.astype(F32)).astype(BF16)


def _dot3(a, b, dims):
    mm = lambda x, y: lax.dot_general(x, y, (dims, ((), ())), preferred_element_type=F32)
    return mm(a[0], b[0]) + (mm(a[0], b[1]) + mm(a[1], b[0]))


def _full(shape):
    return pl.BlockSpec(shape, lambda *_: (0,) * len(shape))


def _pool_kernel(x_ref, halo_ref, g_ref, w_ref, sc_ref, o_ref, *, ts, blocks_per_seq):
    blk = pl.program_id(0) % blocks_per_seq
    x = x_ref[...]
    g = g_ref[...]
    h = _rms(x, g)
    hh = jnp.where(blk == 0, 0.0, _rms(halo_ref[...], g))
    full = jnp.concatenate([hh, h], axis=0)
    t = blk * ts + lax.broadcasted_iota(jnp.int32, (ts, 1), 0)
    gw = w_ref.shape[-1]
    outs = []
    for gi, w in enumerate(POOL_WINDOWS):
        p = full[:, gi * gw:(gi + 1) * gw]
        step = 1
        while step < w:
            p = p + pltpu.roll(p, step, axis=0)
            step *= 2
        win = p[POOL_HALO:, :]
        cnt = jnp.minimum(t + 1, w).astype(F32)
        pooled = win / cnt - h[:, gi * gw:(gi + 1) * gw]
        outs.append(_dot(pooled, w_ref[gi]))
    o_ref[...] = x + jnp.concatenate(outs, axis=-1) * sc_ref[...]


def _pool_layer(x2, seq, g, w, scale):
    n, d = x2.shape
    ts = min(POOL_BLOCK, seq)
    assert seq % ts == 0 and ts % POOL_HALO == 0 and all(wd & (wd - 1) == 0 and wd <= POOL_HALO for wd in POOL_WINDOWS)
    hb = ts // POOL_HALO
    return pl.pallas_call(
        functools.partial(_pool_kernel, ts=ts, blocks_per_seq=seq // ts),
        grid=(n // ts,),
        in_specs=[
            pl.BlockSpec((ts, d), lambda i: (i, 0)),
            pl.BlockSpec((POOL_HALO, d), lambda i: (jnp.maximum(i * hb - 1, 0), 0)),
            _full((1, d)),
            _full(w.shape),
            _full((1, d)),
        ],
        out_specs=pl.BlockSpec((ts, d), lambda i: (i, 0)),
        out_shape=jax.ShapeDtypeStruct((n, d), F32),
        compiler_params=_params(("parallel",)),
        name="pool_mixer",
    )(x2, x2, g.reshape(1, d), w.astype(BF16), scale.reshape(1, d))


def _rope_kernel(pos_ref, inv_ref, cos_ref, sin_ref):
    ang = inv_ref[...] * pos_ref[0].astype(F32)
    cos_ref[0] = jnp.cos(ang)
    sin_ref[0] = jnp.sin(ang)


def _rope_tables(positions):
    b, s = positions.shape
    half = QK_ROPE // 2
    inv = ROPE_THETA ** (-jnp.arange(0, QK_ROPE, 2, dtype=F32) / QK_ROPE)
    cos_t, sin_t = pl.pallas_call(
        _rope_kernel,
        grid=(b,),
        in_specs=[pl.BlockSpec((1, 1, s), lambda i: (i, 0, 0)), _full((half, 1))],
        out_specs=[pl.BlockSpec((1, half, s), lambda i: (i, 0, 0))] * 2,
        out_shape=[jax.ShapeDtypeStruct((b, half, s), F32)] * 2,
        compiler_params=_params(("parallel",)),
        name="rope_tables",
    )(positions.reshape(b, 1, s), inv.reshape(half, 1))
    to_rows = lambda a: a.transpose(0, 2, 1).reshape(b * s, half)
    return to_rows(cos_t), to_rows(sin_t)


def _kv_kernel(x_ref, cos_ref, sin_ref, gin_ref, wdkv_ref, gkv_ref, wukv_ref, k_ref, v_ref):
    h = _rms(x_ref[...], gin_ref[...])
    ckv = _dot(h, wdkv_ref[...])
    c = _rms(ckv[:, :KV_RANK], gkv_ref[...])
    half = QK_ROPE // 2
    x1 = ckv[:, KV_RANK:KV_RANK + half]
    x2 = ckv[:, KV_RANK + half:]
    cos = cos_ref[...]
    sin = sin_ref[...]
    k1 = x1 * cos - x2 * sin
    k2 = x2 * cos + x1 * sin
    kv = _dot(c, wukv_ref[...])
    for hd in range(MLA_HEADS):
        k_ref[hd] = jnp.concatenate([kv[:, hd * QK_NOPE:(hd + 1) * QK_NOPE], k1, k2], axis=-1).astype(k_ref.dtype)
        off = MLA_HEADS * QK_NOPE + hd * V_HEAD
        v_ref[hd] = kv[:, off:off + V_HEAD].astype(v_ref.dtype)


def _split_heads(w, bounds):
    w3 = w.reshape(w.shape[0], MLA_HEADS, -1)
    return jnp.concatenate([w3[:, :, lo:hi].reshape(w.shape[0], -1) for lo, hi in bounds], axis=1)


def _shared_kv(x2, cos, sin, kv_in_norm, w_dkv, kv_norm, w_ukv):
    n, d = x2.shape
    t = ROW_BLOCK
    half = QK_ROPE // 2
    w_ukv_p = _split_heads(w_ukv, [(0, QK_NOPE), (QK_NOPE, QK_NOPE + V_HEAD)])
    row = lambda width: pl.BlockSpec((t, width), lambda i: (i, 0))
    return pl.pallas_call(
        _kv_kernel,
        grid=(n // t,),
        in_specs=[row(d), row(half), row(half), _full((1, d)), _full(w_dkv.shape), _full((1, KV_RANK)),
                  _full(w_ukv_p.shape)],
        out_specs=[pl.BlockSpec((MLA_HEADS, t, QK_NOPE + QK_ROPE), lambda i: (0, i, 0)),
                   pl.BlockSpec((MLA_HEADS, t, V_HEAD), lambda i: (0, i, 0))],
        out_shape=[jax.ShapeDtypeStruct((MLA_HEADS, n, QK_NOPE + QK_ROPE), ATTN_DTYPE),
                   jax.ShapeDtypeStruct((MLA_HEADS, n, V_HEAD), ATTN_DTYPE)],
        compiler_params=_params(("parallel",)),
        name="shared_kv",
    )(x2, cos, sin, kv_in_norm.reshape(1, d), w_dkv.astype(BF16), kv_norm.reshape(1, KV_RANK), w_ukv_p.astype(BF16))


def _q_kernel(x_ref, cos_ref, sin_ref, g_ref, wdq_ref, gq_ref, wuq_ref, q_ref):
    h = _rms(x_ref[...], g_ref[...])
    cq = _rms(_dot(h, wdq_ref[...]), gq_ref[...])
    q = _dot(cq, wuq_ref[...]) * ATTN_SCALE
    half = QK_ROPE // 2
    nope = MLA_HEADS * QK_NOPE
    cos = jnp.concatenate([cos_ref[...]] * MLA_HEADS, axis=-1)
    sin = jnp.concatenate([sin_ref[...]] * MLA_HEADS, axis=-1)
    p1 = q[:, nope:nope + MLA_HEADS * half]
    p2 = q[:, nope + MLA_HEADS * half:]
    r1 = p1 * cos - p2 * sin
    r2 = p2 * cos + p1 * sin
    for hd in range(MLA_HEADS):
        q_ref[hd] = jnp.concatenate([q[:, hd * QK_NOPE:(hd + 1) * QK_NOPE], r1[:, hd * half:(hd + 1) * half],
                                     r2[:, hd * half:(hd + 1) * half]], axis=-1).astype(q_ref.dtype)


def _q_proj(x2, cos, sin, g, w_dq, q_norm, w_uq):
    n, d = x2.shape
    t = ROW_BLOCK
    half = QK_ROPE // 2
    rank = w_dq.shape[1]
    w_uq_p = _split_heads(w_uq, [(0, QK_NOPE), (QK_NOPE, QK_NOPE + half), (QK_NOPE + half, QK_NOPE + QK_ROPE)])
    row = lambda width: pl.BlockSpec((t, width), lambda i: (i, 0))
    return pl.pallas_call(
        _q_kernel,
        grid=(n // t,),
        in_specs=[row(d), row(half), row(half), _full((1, d)), _full(w_dq.shape), _full((1, rank)),
                  _full(w_uq_p.shape)],
        out_specs=pl.BlockSpec((MLA_HEADS, t, QK_NOPE + QK_ROPE), lambda i: (0, i, 0)),
        out_shape=jax.ShapeDtypeStruct((MLA_HEADS, n, QK_NOPE + QK_ROPE), ATTN_DTYPE),
        compiler_params=_params(("parallel",)),
        name="mla_q",
    )(x2, cos, sin, g.reshape(1, d), w_dq.astype(BF16), q_norm.reshape(1, rank), w_uq_p.astype(BF16))


def _flash_kernel(q_ref, k_ref, v_ref, o_ref, m_sc, l_sc, acc_sc, *, tb):
    qi = pl.program_id(2)
    ki = pl.program_id(3)

    @pl.when(ki == 0)
    def _():
        m_sc[...] = jnp.full_like(m_sc, -jnp.inf)
        l_sc[...] = jnp.zeros_like(l_sc)
        acc_sc[...] = jnp.zeros_like(acc_sc)

    def update(masked):
        s = lax.dot_general(q_ref[0], k_ref[0], (((1,), (1,)), ((), ())), preferred_element_type=F32)
        if masked:
            row = lax.broadcasted_iota(jnp.int32, (tb, tb), 0)
            col = lax.broadcasted_iota(jnp.int32, (tb, tb), 1)
            s = jnp.where(col <= row, s, -jnp.inf)
        m_prev = m_sc[...]
        m_new = jnp.maximum(m_prev, jnp.max(s, axis=-1, keepdims=True))
        p = jnp.exp(s - m_new)
        alpha = jnp.exp(m_prev - m_new)
        l_sc[...] = alpha * l_sc[...] + jnp.sum(p, axis=-1, keepdims=True)
        acc_sc[...] = alpha * acc_sc[...] + jnp.dot(p.astype(v_ref.dtype), v_ref[0], preferred_element_type=F32)
        m_sc[...] = m_new

    pl.when(ki < qi)(functools.partial(update, False))
    pl.when(ki == qi)(functools.partial(update, True))

    @pl.when(ki == qi)
    def _():
        o_ref[...] = (acc_sc[...] / l_sc[...]).astype(o_ref.dtype)


def _flash_attention(q, k, v, batch, seq):
    tb = min(ATTN_BLOCK, seq)
    nb = seq // tb
    n = batch * seq
    dk = q.shape[-1]
    return pl.pallas_call(
        functools.partial(_flash_kernel, tb=tb),
        grid=(batch, MLA_HEADS, nb, nb),
        in_specs=[
            pl.BlockSpec((1, tb, dk), lambda b, h, qi, ki: (h, b * nb + qi, 0)),
            pl.BlockSpec((1, tb, dk), lambda b, h, qi, ki: (h, b * nb + jnp.minimum(ki, qi), 0)),
            pl.BlockSpec((1, tb, V_HEAD), lambda b, h, qi, ki: (h, b * nb + jnp.minimum(ki, qi), 0)),
        ],
        out_specs=pl.BlockSpec((tb, V_HEAD), lambda b, h, qi, ki: (b * nb + qi, h)),
        out_shape=jax.ShapeDtypeStruct((n, MLA_HEADS * V_HEAD), ATTN_DTYPE),
        scratch_shapes=[pltpu.VMEM((tb, 1), F32), pltpu.VMEM((tb, 1), F32), pltpu.VMEM((tb, V_HEAD), F32)],
        compiler_params=_params(("parallel", "parallel", "parallel", "arbitrary")),
        name="mla_flash",
    )(q, k, v)


def _oproj_kernel(x_ref, o_ref, w_ref, y_ref):
    y_ref[...] = x_ref[...] + _dot(o_ref[...], w_ref[...])


def _out_proj(x2, o, w_o):
    n, d = x2.shape
    t = ROW_BLOCK
    row = lambda width: pl.BlockSpec((t, width), lambda i: (i, 0))
    return pl.pallas_call(
        _oproj_kernel,
        grid=(n // t,),
        in_specs=[row(d), row(o.shape[1]), _full(w_o.shape)],
        out_specs=row(d),
        out_shape=jax.ShapeDtypeStruct((n, d), F32),
        compiler_params=_params(("parallel",)),
        name="mla_out",
    )(x2, o, w_o.astype(BF16))


def _top_rows(problems, k):
    def pick(s, payload, r, vals, idxs):
        rows, t = s.shape
        iota = lax.broadcasted_iota(jnp.int32, (rows, t), 0)
        out_row = lax.broadcasted_iota(jnp.int32, (k, t), 0)
        m = jnp.max(s, axis=0, keepdims=True)
        pos = jnp.min(jnp.where(s == m, iota, rows), axis=0, keepdims=True)
        hit = iota == pos
        got = pos if payload is None else jnp.sum(jnp.where(hit, payload, 0), axis=0, keepdims=True)
        return (jnp.where(hit, -jnp.inf, s), jnp.where(out_row == r, m, vals), jnp.where(out_row == r, got, idxs))

    def body(r, carry):
        return tuple(pick(s, payload, r, vals, idxs) for (s, vals, idxs), (_, payload) in zip(carry, problems))

    t = problems[0][0].shape[1]
    init = tuple((s, jnp.zeros((k, t), F32), jnp.zeros((k, t), jnp.int32)) for s, _ in problems)
    return [(vals, idxs) for _, vals, idxs in lax.fori_loop(0, k, body, init)]


def _pair_candidates(s1, i1, s2, i2):
    neg = jnp.float32(-jnp.inf)
    sub = lax.broadcasted_iota(jnp.int32, (SUBLANES, s1.shape[1]), 0)
    cs, ci = [s1[0:1] + s2], [i1[0:1] * N_KEYS + i2]
    for a in range(1, PEER_TOPK // 2):
        keep = sub < PEER_TOPK // (a + 1)
        cs.append(jnp.where(keep, s1[a:a + 1] + s2[:SUBLANES], neg))
        ci.append(i1[a:a + 1] * N_KEYS + i2[:SUBLANES])
    cs.append(s1[PEER_TOPK // 2:] + s2[0:1])
    ci.append(i1[PEER_TOPK // 2:] * N_KEYS + i2[0:1])
    return jnp.concatenate(cs, axis=0), jnp.concatenate(ci, axis=0)


def _peer_front_kernel(x_ref, g_ref, wq_hi_ref, wq_lo_ref, keys_hi_ref, keys_lo_ref, hn_ref, gate_ref, idx_rows_ref,
                       gate_rows_ref, idx_sc, gate_sc, *, lane_chunk):
    head = pl.program_id(1)
    head_rows = pl.ds(pl.multiple_of(head * PEER_TOPK, PEER_TOPK), PEER_TOPK)
    hn = _rms(x_ref[...], g_ref[...])

    @pl.when(head == 0)
    def _():
        hn_ref[...] = hn

    q = _dot3(_split(hn), (wq_hi_ref[...], wq_lo_ref[...]), ((1,), (0,)))
    cands = []
    for c in range(hn.shape[0] // lane_chunk):
        tok = slice(c * lane_chunk, (c + 1) * lane_chunk)
        scores = [_dot3((keys_hi_ref[0, half], keys_lo_ref[0, half]),
                        _split(q[tok, half * PEER_DHALF:(half + 1) * PEER_DHALF]), ((1,), (1,))) for half in range(2)]
        (s1, i1), (s2, i2) = _top_rows([(sc, None) for sc in scores], PEER_TOPK)
        cands.append(_pair_candidates(s1, i1, s2, i2))
    for c, (best_s, best_i) in enumerate(_top_rows(cands, PEER_TOPK)):
        tok = slice(c * lane_chunk, (c + 1) * lane_chunk)
        e = jnp.exp(best_s - best_s[0:1, :])
        gate = e / jnp.sum(e, axis=0, keepdims=True)
        gate_ref[0, :, tok] = gate
        gate_sc[head_rows, tok] = gate
        idx_sc[head_rows, tok] = best_i

    @pl.when(head == pl.num_programs(1) - 1)
    def _():
        idx_rows_ref[...] = idx_sc[...].T
        gate_rows_ref[...] = gate_sc[...].T


def _peer_front(x2, g, w_q, keys):
    n, d = x2.shape
    t = ROW_BLOCK
    nb = n // t
    dq = 2 * PEER_DHALF
    k = PEER_HEADS * PEER_TOPK
    return pl.pallas_call(
        functools.partial(_peer_front_kernel, lane_chunk=128),
        grid=(nb, PEER_HEADS),
        in_specs=[
            pl.BlockSpec((t, d), lambda i, h: (i, 0)),
            _full((1, d)),
            pl.BlockSpec((d, dq), lambda i, h: (0, h)),
            pl.BlockSpec((d, dq), lambda i, h: (0, h)),
            pl.BlockSpec((1, 2, N_KEYS, PEER_DHALF), lambda i, h: (h, 0, 0, 0)),
            pl.BlockSpec((1, 2, N_KEYS, PEER_DHALF), lambda i, h: (h, 0, 0, 0)),
        ],
        out_specs=[
            pl.BlockSpec((t, d), lambda i, h: (i, 0)),
            pl.BlockSpec((1, PEER_TOPK, t), lambda i, h: (i, h, 0)),
            pl.BlockSpec((t, k), lambda i, h: (i, 0)),
            pl.BlockSpec((t, k), lambda i, h: (i, 0)),
        ],
        out_shape=[
            jax.ShapeDtypeStruct((n, d), F32),
            jax.ShapeDtypeStruct((nb, k, t), F32),
            jax.ShapeDtypeStruct((n, k), jnp.int32),
            jax.ShapeDtypeStruct((n, k), F32),
        ],
        scratch_shapes=[pltpu.VMEM((k, t), jnp.int32), pltpu.VMEM((k, t), F32)],
        compiler_params=_params(("parallel", "arbitrary")),
        name="peer_front",
    )(x2, g.reshape(1, d), *_split(w_q), *_split(keys))


def _row_sums(p):
    k = p.shape[0]
    p = p.reshape(k // SUBLANES, SUBLANES, SUBLANES, LANES)
    s = lax.broadcasted_iota(jnp.int32, (1, 1, SUBLANES, LANES), 2)
    a, b = p[:, :4], p[:, 4:]
    m = (s & 4) == 0
    p = jnp.where(m, a, b) + pltpu.roll(jnp.where(m, b, a), 4, axis=2)
    for dist in (2, 1):
        h = p.shape[1] // 2
        a, b = p[:, :h], p[:, h:]
        m = (s & dist) == 0
        p = jnp.where(m, a + pltpu.roll(a, SUBLANES - dist, axis=2), b + pltpu.roll(b, dist, axis=2))
    return p.reshape(k, LANES)


def _swap_rows_and_chunks(parts):
    parts = list(parts)
    q = lax.broadcasted_iota(jnp.int32, parts[0].shape, 1)
    for d in (4, 2, 1):
        low = (q & d) == 0
        for p0 in range(SUBLANES):
            if p0 & d:
                continue
            a, b = parts[p0], parts[p0 + d]
            parts[p0] = jnp.where(low, a, pltpu.roll(b, d, axis=1))
            parts[p0 + d] = jnp.where(low, pltpu.roll(a, SUBLANES - d, axis=1), b)
    return parts


def _peer_experts_kernel(idx_ref, idx_next_ref, hn_ref, gate_ref, x_ref, uv_hbm, o_ref, buf_a, buf_b, wbuf, hbuf, obuf, sem,
                         *, sub):
    step = pl.program_id(0)
    last_step = pl.num_programs(0) - 1
    t = hn_ref.shape[0]
    k = gate_ref.shape[1]
    n_sub = t // sub
    bufs = (buf_a, buf_b)
    lane = lax.broadcasted_iota(jnp.int32, (1, t), 1)
    hi_mask = jnp.int32(-65536)

    def issue_token(idx, base, tk, slot):
        for r in range(k):
            e = idx[base + tk * k + r]
            pltpu.make_async_copy(uv_hbm.at[e], bufs[slot].at[tk * k + r], sem.at[slot]).start(priority=r % 2)

    def wait(slot):
        pltpu.make_async_copy(uv_hbm.at[pl.ds(0, sub * k)], bufs[slot], sem.at[slot]).wait()

    def compute_token(sb, tk, slot):
        buf = bufs[slot]
        tok = sb * sub + tk
        u = lax.bitcast_convert_type(buf[pl.ds(tk * k, k)] << 16, F32)
        a = jnp.sum(_row_sums(u * hbuf[tk][None]), axis=-1, keepdims=True)
        gcol = jnp.sum(jnp.where(lane == tok, gate_ref[0], 0.0), axis=-1, keepdims=True)
        wbuf[tk] = jnp.broadcast_to(gcol * jax.nn.gelu(a), (k, LANES))
        acc = [None] * 4
        for r in range(k):
            v = lax.bitcast_convert_type(buf[tk * k + r] & hi_mask, F32)
            term = jnp.broadcast_to(wbuf[tk, r:r + 1, :], (SUBLANES, LANES)) * v
            acc[r % 4] = term if acc[r % 4] is None else acc[r % 4] + term
        obuf[tk] = (acc[0] + acc[1]) + (acc[2] + acc[3])

    def run_sub(sb, slot, next_idx, next_base):
        rows = pl.ds(pl.multiple_of(sb * sub, SUBLANES), sub)
        chunk = lambda j: slice(j * LANES, (j + 1) * LANES)
        hn = hn_ref[rows, :]
        for r, tile in enumerate(_swap_rows_and_chunks([hn[:, chunk(j)][None] for j in range(SUBLANES)])):
            hbuf[r] = tile[0]
        wait(slot)

        def body(tk, carry):
            if next_idx is not None:
                issue_token(next_idx, next_base, tk, 1 - slot)
            compute_token(sb, tk, slot)
            return carry

        lax.fori_loop(0, sub, body, 0)
        for j, part in enumerate(_swap_rows_and_chunks([obuf[r][None] for r in range(sub)])):
            o_ref[rows, chunk(j)] = x_ref[rows, chunk(j)] + part[0]

    @pl.when(step == 0)
    def _():
        lax.fori_loop(0, sub, lambda tk, c: (issue_token(idx_ref, 0, tk, 0), c)[1], 0)

    def pair(sp, carry):
        run_sub(2 * sp, 0, idx_ref, (2 * sp + 1) * sub * k)
        run_sub(2 * sp + 1, 1, idx_ref, (2 * sp + 2) * sub * k)
        return carry

    lax.fori_loop(0, n_sub // 2 - 1, pair, 0)
    run_sub(n_sub - 2, 0, idx_ref, (n_sub - 1) * sub * k)

    @pl.when(step < last_step)
    def _():
        run_sub(n_sub - 1, 1, idx_next_ref, 0)

    @pl.when(step == last_step)
    def _():
        run_sub(n_sub - 1, 1, None, 0)


def _peer_experts(x2, hn, idx_flat, gate_t, uv, nb):
    d = x2.shape[1]
    _, k, t = gate_t.shape
    n = nb * t
    sub = GATHER_SUB
    assert t % (2 * sub) == 0 and sub == SUBLANES and d == SUBLANES * LANES
    row = pl.BlockSpec((t, d), lambda i: (i, 0))
    return pl.pallas_call(
        functools.partial(_peer_experts_kernel, sub=sub),
        grid=(nb,),
        in_specs=[
            pl.BlockSpec((t * k,), lambda i: (i,), memory_space=pltpu.SMEM),
            pl.BlockSpec((t * k,), lambda i: (jnp.minimum(i + 1, nb - 1),), memory_space=pltpu.SMEM),
            row,
            pl.BlockSpec((1, k, t), lambda i: (i, 0, 0)),
            row,
            pl.BlockSpec(memory_space=pl.ANY),
        ],
        out_specs=row,
        out_shape=jax.ShapeDtypeStruct((n, d), F32),
        scratch_shapes=[
            pltpu.VMEM((sub * k, SUBLANES, LANES), jnp.int32),
            pltpu.VMEM((sub * k, SUBLANES, LANES), jnp.int32),
            pltpu.VMEM((sub, k, LANES), F32),
            pltpu.VMEM((sub, SUBLANES, LANES), F32),
            pltpu.VMEM((sub, SUBLANES, LANES), F32),
            pltpu.SemaphoreType.DMA((2,)),
        ],
        compiler_params=_params(("arbitrary",)),
        name="peer_experts",
    )(idx_flat, idx_flat, hn, gate_t, x2, uv)


def _pack_kernel(u_ref, v_ref, o_ref):
    bits = lambda a: lax.bitcast_convert_type(a.astype(BF16).astype(F32), jnp.int32)
    words = (bits(v_ref[...]) & jnp.int32(-65536)) | lax.shift_right_logical(bits(u_ref[...]), 16)
    g = words.shape[0] // SUBLANES
    parts = [words[:, j * LANES:(j + 1) * LANES].reshape(g, SUBLANES, LANES) for j in range(SUBLANES)]
    for r, tile in enumerate(_swap_rows_and_chunks(parts)):
        o_ref[:, r] = tile


def _pack_experts(u, v):
    e, d = u.shape
    t = ROW_BLOCK
    assert d == SUBLANES * LANES and e % t == 0
    out = pl.pallas_call(
        _pack_kernel,
        grid=(e // t,),
        in_specs=[pl.BlockSpec((t, d), lambda i: (i, 0))] * 2,
        out_specs=pl.BlockSpec((t // SUBLANES, SUBLANES, SUBLANES, LANES), lambda i: (i, 0, 0, 0)),
        out_shape=jax.ShapeDtypeStruct((e // SUBLANES, SUBLANES, SUBLANES, LANES), jnp.int32),
        compiler_params=_params(("parallel",)),
        name="pack_experts",
    )(u, v)
    return out.reshape(e, SUBLANES, LANES)


def _pack_halves(t):
    half = t.shape[1] // 2
    bits = lambda a: lax.bitcast_convert_type(a.astype(BF16), jnp.uint16).astype(jnp.uint32)
    return lax.bitcast_convert_type((bits(t[:, half:]) << 16) | bits(t[:, :half]), jnp.int32)


def _sc_peer_experts(idx, hn, gate, x, u16, v16, first, n):
    k = idx.shape[1]
    d = hn.shape[1]
    half = d // 2
    ln = SC_LANES
    workers = SC_CORES * SC_SUBCORES
    nch = k // SC_CHUNK
    ntask = 2 * nch
    assert n % (workers * SC_TOKENS) == 0 and half % (ln * SC_ACC_WORDS) == 0 and SC_CHUNK % SC_ROW_GROUP == 0
    assert ntask % SC_SLOTS == 0 and SC_AHEAD < SC_SLOTS
    per_w = n // workers
    mesh = plsc.VectorSubcoreMesh(core_axis_name="c", subcore_axis_name="s")
    hi_mask = jnp.int32(-65536)

    def words(w):
        return lax.bitcast_convert_type(w << 16, F32), lax.bitcast_convert_type(w & hi_mask, F32)

    @functools.partial(
        pl.kernel, mesh=mesh, out_type=jax.ShapeDtypeStruct((n, d), F32),
        scratch_types=[
            pltpu.VMEM((SC_TOKENS, k), jnp.int32),
            pltpu.VMEM((SC_TOKENS, d), F32),
            pltpu.VMEM((SC_TOKENS, k), F32),
            pltpu.VMEM((SC_TOKENS, d), F32),
            [pltpu.VMEM((SC_CHUNK, half), jnp.int32)] * SC_SLOTS,
            pltpu.VMEM((k, ln), F32),
            pltpu.VMEM((k,), F32),
            [pltpu.SemaphoreType.DMA] * SC_SLOTS,
        ],
        compiler_params=pltpu.CompilerParams(needs_layout_passes=False, use_tc_tiling_on_sc=True),
        name="peer_experts_sc",
    )
    def kern(idx_hbm, hn_hbm, gate_hbm, x_hbm, u_hbm, v_hbm, o_hbm, idx_v, hn_v, gate_v, x_v, ring, part, w_v, sems):
        wid = lax.axis_index("s") * SC_CORES + lax.axis_index("c")
        lanes = lax.iota(jnp.int32, ln)

        def chunk_copy(ti, task):
            table = u_hbm if task < nch else v_hbm
            rows = idx_v.at[ti, pl.ds((task % nch) * SC_CHUNK, SC_CHUNK)]
            return pltpu.make_async_copy(table.at[rows], ring[task % SC_SLOTS], sems[task % SC_SLOTS])

        def down_chunk(ti, c, buf):
            @pl.loop(0, SC_CHUNK // SC_ROW_GROUP)
            def _(rg):
                r0 = rg * SC_ROW_GROUP
                lo_acc = [jnp.zeros((ln,), F32) for _ in range(SC_ROW_GROUP)]
                hi_acc = [jnp.zeros((ln,), F32) for _ in range(SC_ROW_GROUP)]
                for j in range(half // ln):
                    xl = hn_v[ti, pl.ds(j * ln, ln)]
                    xh = hn_v[ti, pl.ds(half + j * ln, ln)]
                    for i in range(SC_ROW_GROUP):
                        lo, hi = words(buf[r0 + i, pl.ds(j * ln, ln)])
                        lo_acc[i] = lo_acc[i] + lo * xl
                        hi_acc[i] = hi_acc[i] + hi * xh
                for i in range(SC_ROW_GROUP):
                    part[c * SC_CHUNK + r0 + i, :] = lo_acc[i] + hi_acc[i]

        def expert_weights(ti):
            @pl.loop(0, k // ln)
            def _(g):
                rows = g * ln + lanes
                a = jnp.zeros((ln,), F32)
                for l in range(ln):
                    a = a + plsc.load_gather(part, [rows, jnp.full((ln,), l, jnp.int32)])
                z = math.sqrt(2.0 / math.pi) * (a + 0.044715 * a * a * a)
                tanh = 1.0 - 2.0 / (jnp.exp(2.0 * z) + 1.0)
                w_v[pl.ds(g * ln, ln)] = gate_v[ti, pl.ds(g * ln, ln)] * (0.5 * a * (1.0 + tanh))

        def up_chunk(ti, c, buf):
            for jb in range(half // ln // SC_ACC_WORDS):
                cols = [(jb * SC_ACC_WORDS + jj) * ln for jj in range(SC_ACC_WORDS)]
                init = []
                for col in cols:
                    init += [x_v[ti, pl.ds(col, ln)], x_v[ti, pl.ds(half + col, ln)]]

                def body(r, accs, cols=cols):
                    ws = plsc.load_gather(w_v, [jnp.zeros((ln,), jnp.int32) + (c * SC_CHUNK + r)])
                    out = []
                    for jj, col in enumerate(cols):
                        lo, hi = words(buf[r, pl.ds(col, ln)])
                        out += [accs[2 * jj] + ws * lo, accs[2 * jj + 1] + ws * hi]
                    return tuple(out)

                accs = lax.fori_loop(0, SC_CHUNK, body, tuple(init))
                for jj, col in enumerate(cols):
                    x_v[ti, pl.ds(col, ln)] = accs[2 * jj]
                    x_v[ti, pl.ds(half + col, ln)] = accs[2 * jj + 1]

        @pl.loop(0, per_w // SC_TOKENS)
        def _(bi):
            out_base = wid * per_w + bi * SC_TOKENS
            base = first + out_base
            pltpu.sync_copy(idx_hbm.at[pl.ds(base, SC_TOKENS)], idx_v)
            pltpu.sync_copy(hn_hbm.at[pl.ds(base, SC_TOKENS)], hn_v)
            pltpu.sync_copy(gate_hbm.at[pl.ds(base, SC_TOKENS)], gate_v)
            pltpu.sync_copy(x_hbm.at[pl.ds(base, SC_TOKENS)], x_v)
            for task in range(SC_AHEAD):
                chunk_copy(0, task).start()

            @pl.loop(0, SC_TOKENS)
            def _(ti):
                for task in range(ntask):
                    chunk_copy(ti, task).wait()
                    ahead = task + SC_AHEAD
                    if ahead < ntask:
                        chunk_copy(ti, ahead).start()
                    else:
                        @pl.when(ti + 1 < SC_TOKENS)
                        def _(ahead=ahead):
                            chunk_copy(ti + 1, ahead - ntask).start()
                    buf = ring[task % SC_SLOTS]
                    if task < nch:
                        down_chunk(ti, task, buf)
                        if task == nch - 1:
                            expert_weights(ti)
                    else:
                        up_chunk(ti, task - nch, buf)

            pltpu.sync_copy(x_v, o_hbm.at[pl.ds(out_base, SC_TOKENS)])

    return kern(idx, hn, gate, x, u16, v16)


def _peer_layer(x2, g, w_q, keys, u, v):
    hn, gate_t, idx_rows, gate_rows = _peer_front(x2, g, w_q, keys)
    nb, k, t = gate_t.shape
    n = nb * t
    nb_sc = (nb * SC_SHARE_NUM) // SC_SHARE_DEN
    n_tc = (nb - nb_sc) * t
    out_tc = _peer_experts(x2, hn, idx_rows.reshape(n * k), gate_t, _pack_experts(u, v), nb - nb_sc)
    if nb_sc == 0:
        return out_tc
    out_sc = _sc_peer_experts(idx_rows, hn, gate_rows, x2, _pack_halves(u), _pack_halves(v), n_tc, n - n_tc)
    return jnp.concatenate([out_tc, out_sc], axis=0)


def _ple_kernel(x_ref, p_ref, g_ref, wg_ref, wp_ref, fg_ref, o_ref, *, final):
    x = x_ref[...]
    gate = jax.nn.sigmoid(_dot(_rms(x, g_ref[...]), wg_ref[...]))
    y = x + gate * _dot(p_ref[...], wp_ref[...])
    o_ref[...] = _rms(y, fg_ref[...]) if final else y


def _ple_layer(x2, p2, g, w_g, w_p, final_g, final):
    n, d = x2.shape
    t = ROW_BLOCK
    row = lambda width: pl.BlockSpec((t, width), lambda i: (i, 0))
    return pl.pallas_call(
        functools.partial(_ple_kernel, final=final),
        grid=(n // t,),
        in_specs=[row(d), row(p2.shape[1]), _full((1, d)), _full(w_g.shape), _full(w_p.shape), _full((1, d))],
        out_specs=row(d),
        out_shape=jax.ShapeDtypeStruct((n, d), F32),
        compiler_params=_params(("parallel",)),
        name="ple",
    )(x2, p2, g.reshape(1, d), w_g.astype(BF16), w_p.astype(BF16), final_g.reshape(1, d))


def kernel(x, p, positions, mix_norm, ffn_norm, pool_w, pool_scale, kv_in_norm, w_dkv, kv_norm, w_ukv, w_dq, q_norm, w_uq, w_o, peer_wq, peer_keys, peer_u, peer_v, ple_norm, ple_wg, ple_wp, final_norm):
    weights = (mix_norm, ffn_norm, pool_w, pool_scale, kv_in_norm, w_dkv, kv_norm, w_ukv, w_dq, q_norm, w_uq, w_o,
               peer_wq, peer_keys, peer_u, peer_v, ple_norm, ple_wg, ple_wp, final_norm)
    per = x.shape[0] // BATCH_CHAINS
    assert per * BATCH_CHAINS == x.shape[0]
    outs = [_trunk(x[c * per:(c + 1) * per], p[:, c * per:(c + 1) * per], positions[c * per:(c + 1) * per], *weights)
            for c in range(BATCH_CHAINS)]
    return jnp.concatenate(outs, axis=0)


def _trunk(x, p, positions, mix_norm, ffn_norm, pool_w, pool_scale, kv_in_norm, w_dkv, kv_norm, w_ukv, w_dq, q_norm, w_uq, w_o, peer_wq, peer_keys, peer_u, peer_v, ple_norm, ple_wg, ple_wp, final_norm):
    b, s, d = x.shape
    depth = p.shape[0]
    n_a = pool_w.shape[0]
    x2 = x.reshape(b * s, d)
    cos, sin = _rope_tables(positions)
    k_sh = v_sh = None
    for i in range(depth):
        if i < n_a:
            x2 = _pool_layer(x2, s, mix_norm[i], pool_w[i], pool_scale[i])
        else:
            if i == n_a:
                k_sh, v_sh = _shared_kv(x2, cos, sin, kv_in_norm, w_dkv, kv_norm, w_ukv)
            j = i - n_a
            q = _q_proj(x2, cos, sin, mix_norm[i], w_dq[j], q_norm[j], w_uq[j])
            o = _flash_attention(q, k_sh, v_sh, b, s)
            x2 = _out_proj(x2, o, w_o[j])
        x2 = _peer_layer(x2, ffn_norm[i], peer_wq[i], peer_keys[i], peer_u[i], peer_v[i])
        x2 = _ple_layer(x2, p[i].reshape(b * s, -1), ple_norm[i], ple_wg[i], ple_wp[i], final_norm, i == depth - 1)
    return x2.reshape(b, s, d)
```

```python
import functools
import math

import jax
import jax.numpy as jnp
from jax import lax
from jax.experimental import pallas as pl
from jax.experimental.pallas import tpu as pltpu
from jax.experimental.pallas import tpu_sc as plsc

F32 = jnp.float32
BF16 = jnp.bfloat16
EPS = 1e-6

POOL_WINDOWS = (2, 4, 8, 16)
POOL_HALO = 16
MLA_HEADS = 8
QK_NOPE = 128
QK_ROPE = 64
V_HEAD = 128
KV_RANK = 256
ROPE_THETA = 10000.0
ATTN_SCALE = 1.0 / math.sqrt(QK_NOPE + QK_ROPE)
PEER_HEADS = 8
N_KEYS = 128
PEER_TOPK = 16
PEER_DHALF = 128

SUBLANES = 8
LANES = 128
ROW_BLOCK = 256
PEER_BLOCK = 512
POOL_BLOCK = 512
ATTN_BLOCK = 512
ATTN_DTYPE = jnp.bfloat16
GATHER_SUB = 8
VMEM_LIMIT = 56 * 1024 * 1024
SC_CORES = 2
SC_SUBCORES = 16
SC_LANES = 16
SC_TOKENS = 16
SC_CHUNK = 32
SC_SLOTS = 4
SC_AHEAD = 3
SC_ROW_GROUP = 4
SC_ACC_WORDS = 8
SC_SHARE_NUM, SC_SHARE_DEN = (48, 48, 60, 50), 64
BATCH_CHAINS = 2


def _params(sem):
    return pltpu.CompilerParams(dimension_semantics=sem, vmem_limit_bytes=VMEM_LIMIT)


def _rms(x, g):
    ms = jnp.mean(x * x, axis=-1, keepdims=True)
    return x * lax.rsqrt(ms + EPS) * g


def _dot(a, w):
    return jnp.dot(a.astype(w.dtype), w, preferred_element_type=F32)


def _split(a):
    hi = a.astype(BF16)
    return hi, (a - hi.astype(F32)).astype(BF16)


def _dot3(a, b, dims):
    mm = lambda x, y: lax.dot_general(x, y, (dims, ((), ())), preferred_element_type=F32)
    return mm(a[0], b[0]) + (mm(a[0], b[1]) + mm(a[1], b[0]))


def _full(shape):
    return pl.BlockSpec(shape, lambda *_: (0,) * len(shape))


def _pool_kernel(x_ref, halo_ref, g_ref, w_ref, sc_ref, o_ref, *, ts, blocks_per_seq):
    blk = pl.program_id(0) % blocks_per_seq
    x = x_ref[...]
    g = g_ref[...]
    h = _rms(x, g)
    hh = jnp.where(blk == 0, 0.0, _rms(halo_ref[...], g))
    full = jnp.concatenate([hh, h], axis=0)
    t = blk * ts + lax.broadcasted_iota(jnp.int32, (ts, 1), 0)
    gw = w_ref.shape[-1]
    outs = []
    for gi, w in enumerate(POOL_WINDOWS):
        p = full[:, gi * gw:(gi + 1) * gw]
        step = 1
        while step < w:
            p = p + pltpu.roll(p, step, axis=0)
            step *= 2
        win = p[POOL_HALO:, :]
        cnt = jnp.minimum(t + 1, w).astype(F32)
        pooled = win / cnt - h[:, gi * gw:(gi + 1) * gw]
        outs.append(_dot(pooled, w_ref[gi]))
    o_ref[...] = x + jnp.concatenate(outs, axis=-1) * sc_ref[...]


def _pool_layer(x2, seq, g, w, scale):
    n, d = x2.shape
    ts = min(POOL_BLOCK, seq)
    assert seq % ts == 0 and ts % POOL_HALO == 0 and all(wd & (wd - 1) == 0 and wd <= POOL_HALO for wd in POOL_WINDOWS)
    hb = ts // POOL_HALO
    return pl.pallas_call(
        functools.partial(_pool_kernel, ts=ts, blocks_per_seq=seq // ts),
        grid=(n // ts,),
        in_specs=[
            pl.BlockSpec((ts, d), lambda i: (i, 0)),
            pl.BlockSpec((POOL_HALO, d), lambda i: (jnp.maximum(i * hb - 1, 0), 0)),
            _full((1, d)),
            _full(w.shape),
            _full((1, d)),
        ],
        out_specs=pl.BlockSpec((ts, d), lambda i: (i, 0)),
        out_shape=jax.ShapeDtypeStruct((n, d), F32),
        compiler_params=_params(("parallel",)),
        name="pool_mixer",
    )(x2, x2, g.reshape(1, d), w.astype(BF16), scale.reshape(1, d))


def _rope_kernel(pos_ref, inv_ref, cos_ref, sin_ref):
    ang = inv_ref[...] * pos_ref[0].astype(F32)
    cos_ref[0] = jnp.cos(ang)
    sin_ref[0] = jnp.sin(ang)


def _rope_tables(positions):
    b, s = positions.shape
    half = QK_ROPE // 2
    inv = ROPE_THETA ** (-jnp.arange(0, QK_ROPE, 2, dtype=F32) / QK_ROPE)
    cos_t, sin_t = pl.pallas_call(
        _rope_kernel,
        grid=(b,),
        in_specs=[pl.BlockSpec((1, 1, s), lambda i: (i, 0, 0)), _full((half, 1))],
        out_specs=[pl.BlockSpec((1, half, s), lambda i: (i, 0, 0))] * 2,
        out_shape=[jax.ShapeDtypeStruct((b, half, s), F32)] * 2,
        compiler_params=_params(("parallel",)),
        name="rope_tables",
    )(positions.reshape(b, 1, s), inv.reshape(half, 1))
    to_rows = lambda a: a.transpose(0, 2, 1).reshape(b * s, half)
    return to_rows(cos_t), to_rows(sin_t)


def _kv_kernel(x_ref, cos_ref, sin_ref, gin_ref, wdkv_ref, gkv_ref, wukv_ref, k_ref, v_ref):
    h = _rms(x_ref[...], gin_ref[...])
    ckv = _dot(h, wdkv_ref[...])
    c = _rms(ckv[:, :KV_RANK], gkv_ref[...])
    half = QK_ROPE // 2
    x1 = ckv[:, KV_RANK:KV_RANK + half]
    x2 = ckv[:, KV_RANK + half:]
    cos = cos_ref[...]
    sin = sin_ref[...]
    k1 = x1 * cos - x2 * sin
    k2 = x2 * cos + x1 * sin
    kv = _dot(c, wukv_ref[...])
    for hd in range(MLA_HEADS):
        k_ref[hd] = jnp.concatenate([kv[:, hd * QK_NOPE:(hd + 1) * QK_NOPE], k1, k2], axis=-1).astype(k_ref.dtype)
        off = MLA_HEADS * QK_NOPE + hd * V_HEAD
        v_ref[hd] = kv[:, off:off + V_HEAD].astype(v_ref.dtype)


def _split_heads(w, bounds):
    w3 = w.reshape(w.shape[0], MLA_HEADS, -1)
    return jnp.concatenate([w3[:, :, lo:hi].reshape(w.shape[0], -1) for lo, hi in bounds], axis=1)


def _shared_kv(x2, cos, sin, kv_in_norm, w_dkv, kv_norm, w_ukv):
    n, d = x2.shape
    t = ROW_BLOCK
    half = QK_ROPE // 2
    w_ukv_p = _split_heads(w_ukv, [(0, QK_NOPE), (QK_NOPE, QK_NOPE + V_HEAD)])
    row = lambda width: pl.BlockSpec((t, width), lambda i: (i, 0))
    return pl.pallas_call(
        _kv_kernel,
        grid=(n // t,),
        in_specs=[row(d), row(half), row(half), _full((1, d)), _full(w_dkv.shape), _full((1, KV_RANK)),
                  _full(w_ukv_p.shape)],
        out_specs=[pl.BlockSpec((MLA_HEADS, t, QK_NOPE + QK_ROPE), lambda i: (0, i, 0)),
                   pl.BlockSpec((MLA_HEADS, t, V_HEAD), lambda i: (0, i, 0))],
        out_shape=[jax.ShapeDtypeStruct((MLA_HEADS, n, QK_NOPE + QK_ROPE), ATTN_DTYPE),
                   jax.ShapeDtypeStruct((MLA_HEADS, n, V_HEAD), ATTN_DTYPE)],
        compiler_params=_params(("parallel",)),
        name="shared_kv",
    )(x2, cos, sin, kv_in_norm.reshape(1, d), w_dkv.astype(BF16), kv_norm.reshape(1, KV_RANK), w_ukv_p.astype(BF16))


def _q_kernel(x_ref, cos_ref, sin_ref, g_ref, wdq_ref, gq_ref, wuq_ref, q_ref):
    h = _rms(x_ref[...], g_ref[...])
    cq = _rms(_dot(h, wdq_ref[...]), gq_ref[...])
    q = _dot(cq, wuq_ref[...]) * ATTN_SCALE
    half = QK_ROPE // 2
    nope = MLA_HEADS * QK_NOPE
    cos = jnp.concatenate([cos_ref[...]] * MLA_HEADS, axis=-1)
    sin = jnp.concatenate([sin_ref[...]] * MLA_HEADS, axis=-1)
    p1 = q[:, nope:nope + MLA_HEADS * half]
    p2 = q[:, nope + MLA_HEADS * half:]
    r1 = p1 * cos - p2 * sin
    r2 = p2 * cos + p1 * sin
    for hd in range(MLA_HEADS):
        q_ref[hd] = jnp.concatenate([q[:, hd * QK_NOPE:(hd + 1) * QK_NOPE], r1[:, hd * half:(hd + 1) * half],
                                     r2[:, hd * half:(hd + 1) * half]], axis=-1).astype(q_ref.dtype)


def _q_proj(x2, cos, sin, g, w_dq, q_norm, w_uq):
    n, d = x2.shape
    t = ROW_BLOCK
    half = QK_ROPE // 2
    rank = w_dq.shape[1]
    w_uq_p = _split_heads(w_uq, [(0, QK_NOPE), (QK_NOPE, QK_NOPE + half), (QK_NOPE + half, QK_NOPE + QK_ROPE)])
    row = lambda width: pl.BlockSpec((t, width), lambda i: (i, 0))
    return pl.pallas_call(
        _q_kernel,
        grid=(n // t,),
        in_specs=[row(d), row(half), row(half), _full((1, d)), _full(w_dq.shape), _full((1, rank)),
                  _full(w_uq_p.shape)],
        out_specs=pl.BlockSpec((MLA_HEADS, t, QK_NOPE + QK_ROPE), lambda i: (0, i, 0)),
        out_shape=jax.ShapeDtypeStruct((MLA_HEADS, n, QK_NOPE + QK_ROPE), ATTN_DTYPE),
        compiler_params=_params(("parallel",)),
        name="mla_q",
    )(x2, cos, sin, g.reshape(1, d), w_dq.astype(BF16), q_norm.reshape(1, rank), w_uq_p.astype(BF16))


def _flash_kernel(q_ref, k_ref, v_ref, o_ref, m_sc, l_sc, acc_sc, *, tb):
    qi = pl.program_id(2)
    ki = pl.program_id(3)

    @pl.when(ki == 0)
    def _():
        m_sc[...] = jnp.full_like(m_sc, -jnp.inf)
        l_sc[...] = jnp.zeros_like(l_sc)
        acc_sc[...] = jnp.zeros_like(acc_sc)

    def update(masked):
        s = lax.dot_general(q_ref[0], k_ref[0], (((1,), (1,)), ((), ())), preferred_element_type=F32)
        if masked:
            row = lax.broadcasted_iota(jnp.int32, (tb, tb), 0)
            col = lax.broadcasted_iota(jnp.int32, (tb, tb), 1)
            s = jnp.where(col <= row, s, -jnp.inf)
        m_prev = m_sc[...]
        m_new = jnp.maximum(m_prev, jnp.max(s, axis=-1, keepdims=True))
        p = jnp.exp(s - m_new)
        alpha = jnp.exp(m_prev - m_new)
        l_sc[...] = alpha * l_sc[...] + jnp.sum(p, axis=-1, keepdims=True)
        acc_sc[...] = alpha * acc_sc[...] + jnp.dot(p.astype(v_ref.dtype), v_ref[0], preferred_element_type=F32)
        m_sc[...] = m_new

    pl.when(ki < qi)(functools.partial(update, False))
    pl.when(ki == qi)(functools.partial(update, True))

    @pl.when(ki == qi)
    def _():
        o_ref[...] = (acc_sc[...] / l_sc[...]).astype(o_ref.dtype)


def _flash_attention(q, k, v, batch, seq):
    tb = min(ATTN_BLOCK, seq)
    nb = seq // tb
    n = batch * seq
    dk = q.shape[-1]
    return pl.pallas_call(
        functools.partial(_flash_kernel, tb=tb),
        grid=(batch, MLA_HEADS, nb, nb),
        in_specs=[
            pl.BlockSpec((1, tb, dk), lambda b, h, qi, ki: (h, b * nb + qi, 0)),
            pl.BlockSpec((1, tb, dk), lambda b, h, qi, ki: (h, b * nb + jnp.minimum(ki, qi), 0)),
            pl.BlockSpec((1, tb, V_HEAD), lambda b, h, qi, ki: (h, b * nb + jnp.minimum(ki, qi), 0)),
        ],
        out_specs=pl.BlockSpec((tb, V_HEAD), lambda b, h, qi, ki: (b * nb + qi, h)),
        out_shape=jax.ShapeDtypeStruct((n, MLA_HEADS * V_HEAD), ATTN_DTYPE),
        scratch_shapes=[pltpu.VMEM((tb, 1), F32), pltpu.VMEM((tb, 1), F32), pltpu.VMEM((tb, V_HEAD), F32)],
        compiler_params=_params(("parallel", "parallel", "parallel", "arbitrary")),
        name="mla_flash",
    )(q, k, v)


def _oproj_kernel(x_ref, o_ref, w_ref, y_ref):
    y_ref[...] = x_ref[...] + _dot(o_ref[...], w_ref[...])


def _out_proj(x2, o, w_o):
    n, d = x2.shape
    t = ROW_BLOCK
    row = lambda width: pl.BlockSpec((t, width), lambda i: (i, 0))
    return pl.pallas_call(
        _oproj_kernel,
        grid=(n // t,),
        in_specs=[row(d), row(o.shape[1]), _full(w_o.shape)],
        out_specs=row(d),
        out_shape=jax.ShapeDtypeStruct((n, d), F32),
        compiler_params=_params(("parallel",)),
        name="mla_out",
    )(x2, o, w_o.astype(BF16))


def _top_rows(problems, k):
    def pick(s, payload, r, vals, idxs):
        rows, t = s.shape
        iota = lax.broadcasted_iota(jnp.int32, (rows, t), 0)
        out_row = lax.broadcasted_iota(jnp.int32, (k, t), 0)
        m = jnp.max(s, axis=0, keepdims=True)
        pos = jnp.min(jnp.where(s == m, iota, rows), axis=0, keepdims=True)
        hit = iota == pos
        got = pos if payload is None else jnp.sum(jnp.where(hit, payload, 0), axis=0, keepdims=True)
        return (jnp.where(hit, -jnp.inf, s), jnp.where(out_row == r, m, vals), jnp.where(out_row == r, got, idxs))

    def body(r, carry):
        return tuple(pick(s, payload, r, vals, idxs) for (s, vals, idxs), (_, payload) in zip(carry, problems))

    t = problems[0][0].shape[1]
    init = tuple((s, jnp.zeros((k, t), F32), jnp.zeros((k, t), jnp.int32)) for s, _ in problems)
    return [(vals, idxs) for _, vals, idxs in lax.fori_loop(0, k, body, init)]


def _pair_candidates(s1, i1, s2, i2):
    neg = jnp.float32(-jnp.inf)
    sub = lax.broadcasted_iota(jnp.int32, (SUBLANES, s1.shape[1]), 0)
    cs, ci = [s1[0:1] + s2], [i1[0:1] * N_KEYS + i2]
    for a in range(1, PEER_TOPK // 2):
        keep = sub < PEER_TOPK // (a + 1)
        cs.append(jnp.where(keep, s1[a:a + 1] + s2[:SUBLANES], neg))
        ci.append(i1[a:a + 1] * N_KEYS + i2[:SUBLANES])
    cs.append(s1[PEER_TOPK // 2:] + s2[0:1])
    ci.append(i1[PEER_TOPK // 2:] * N_KEYS + i2[0:1])
    return jnp.concatenate(cs, axis=0), jnp.concatenate(ci, axis=0)


def _peer_front_kernel(x_ref, g_ref, wq_hi_ref, wq_lo_ref, keys_hi_ref, keys_lo_ref, hn_ref, gate_ref, idx_rows_ref,
                       gate_rows_ref, idx_sc, gate_sc, *, lane_chunk):
    head = pl.program_id(1)
    head_rows = pl.ds(pl.multiple_of(head * PEER_TOPK, PEER_TOPK), PEER_TOPK)
    hn = _rms(x_ref[...], g_ref[...])

    @pl.when(head == 0)
    def _():
        hn_ref[...] = hn

    q = _dot3(_split(hn), (wq_hi_ref[...], wq_lo_ref[...]), ((1,), (0,)))
    cands = []
    for c in range(hn.shape[0] // lane_chunk):
        tok = slice(c * lane_chunk, (c + 1) * lane_chunk)
        scores = [_dot3((keys_hi_ref[0, half], keys_lo_ref[0, half]),
                        _split(q[tok, half * PEER_DHALF:(half + 1) * PEER_DHALF]), ((1,), (1,))) for half in range(2)]
        (s1, i1), (s2, i2) = _top_rows([(sc, None) for sc in scores], PEER_TOPK)
        cands.append(_pair_candidates(s1, i1, s2, i2))
    best = []
    for c in range(0, len(cands), 2):
        best += _top_rows(cands[c:c + 2], PEER_TOPK)
    for c, (best_s, best_i) in enumerate(best):
        tok = slice(c * lane_chunk, (c + 1) * lane_chunk)
        e = jnp.exp(best_s - best_s[0:1, :])
        gate = e / jnp.sum(e, axis=0, keepdims=True)
        gate_ref[0, :, tok] = gate
        gate_sc[head_rows, tok] = gate
        idx_sc[head_rows, tok] = best_i

    @pl.when(head == pl.num_programs(1) - 1)
    def _():
        idx_rows_ref[...] = idx_sc[...].T
        gate_rows_ref[...] = gate_sc[...].T


def _peer_front(x2, g, w_q, keys):
    n, d = x2.shape
    t = PEER_BLOCK
    nb = n // t
    dq = 2 * PEER_DHALF
    k = PEER_HEADS * PEER_TOPK
    return pl.pallas_call(
        functools.partial(_peer_front_kernel, lane_chunk=128),
        grid=(nb, PEER_HEADS),
        in_specs=[
            pl.BlockSpec((t, d), lambda i, h: (i, 0)),
            _full((1, d)),
            pl.BlockSpec((d, dq), lambda i, h: (0, h)),
            pl.BlockSpec((d, dq), lambda i, h: (0, h)),
            pl.BlockSpec((1, 2, N_KEYS, PEER_DHALF), lambda i, h: (h, 0, 0, 0)),
            pl.BlockSpec((1, 2, N_KEYS, PEER_DHALF), lambda i, h: (h, 0, 0, 0)),
        ],
        out_specs=[
            pl.BlockSpec((t, d), lambda i, h: (i, 0)),
            pl.BlockSpec((1, PEER_TOPK, t), lambda i, h: (i, h, 0)),
            pl.BlockSpec((t, k), lambda i, h: (i, 0)),
            pl.BlockSpec((t, k), lambda i, h: (i, 0)),
        ],
        out_shape=[
            jax.ShapeDtypeStruct((n, d), F32),
            jax.ShapeDtypeStruct((nb, k, t), F32),
            jax.ShapeDtypeStruct((n, k), jnp.int32),
            jax.ShapeDtypeStruct((n, k), F32),
        ],
        scratch_shapes=[pltpu.VMEM((k, t), jnp.int32), pltpu.VMEM((k, t), F32)],
        compiler_params=_params(("parallel", "arbitrary")),
        name="peer_front",
    )(x2, g.reshape(1, d), *_split(w_q), *_split(keys))


def _row_sums(p):
    k = p.shape[0]
    p = p.reshape(k // SUBLANES, SUBLANES, SUBLANES, LANES)
    s = lax.broadcasted_iota(jnp.int32, (1, 1, SUBLANES, LANES), 2)
    a, b = p[:, :4], p[:, 4:]
    m = (s & 4) == 0
    p = jnp.where(m, a, b) + pltpu.roll(jnp.where(m, b, a), 4, axis=2)
    for dist in (2, 1):
        h = p.shape[1] // 2
        a, b = p[:, :h], p[:, h:]
        m = (s & dist) == 0
        p = jnp.where(m, a + pltpu.roll(a, SUBLANES - dist, axis=2), b + pltpu.roll(b, dist, axis=2))
    return p.reshape(k, LANES)


def _swap_rows_and_chunks(parts):
    parts = list(parts)
    q = lax.broadcasted_iota(jnp.int32, parts[0].shape, 1)
    for d in (4, 2, 1):
        low = (q & d) == 0
        for p0 in range(SUBLANES):
            if p0 & d:
                continue
            a, b = parts[p0], parts[p0 + d]
            parts[p0] = jnp.where(low, a, pltpu.roll(b, d, axis=1))
            parts[p0 + d] = jnp.where(low, pltpu.roll(a, SUBLANES - d, axis=1), b)
    return parts


def _peer_experts_kernel(idx_ref, idx_next_ref, hn_ref, gate_ref, x_ref, uv_hbm, o_ref, buf_a, buf_b, wbuf, hbuf, obuf, sem,
                         *, sub):
    step = pl.program_id(0)
    last_step = pl.num_programs(0) - 1
    t = hn_ref.shape[0]
    k = gate_ref.shape[1]
    n_sub = t // sub
    bufs = (buf_a, buf_b)
    lane = lax.broadcasted_iota(jnp.int32, (1, t), 1)
    hi_mask = jnp.int32(-65536)

    def issue_token(idx, base, tk, slot):
        for r in range(k):
            e = idx[base + tk * k + r]
            pltpu.make_async_copy(uv_hbm.at[e], bufs[slot].at[tk * k + r], sem.at[slot]).start(priority=r % 2)

    def wait(slot):
        pltpu.make_async_copy(uv_hbm.at[pl.ds(0, sub * k)], bufs[slot], sem.at[slot]).wait()

    def compute_token(sb, tk, slot):
        buf = bufs[slot]
        tok = sb * sub + tk
        u = lax.bitcast_convert_type(buf[pl.ds(tk * k, k)] << 16, F32)
        a = jnp.sum(_row_sums(u * hbuf[tk][None]), axis=-1, keepdims=True)
        gcol = jnp.sum(jnp.where(lane == tok, gate_ref[0], 0.0), axis=-1, keepdims=True)
        wbuf[tk] = jnp.broadcast_to(gcol * jax.nn.gelu(a), (k, LANES))
        acc = [None] * 4
        for r in range(k):
            v = lax.bitcast_convert_type(buf[tk * k + r] & hi_mask, F32)
            term = jnp.broadcast_to(wbuf[tk, r:r + 1, :], (SUBLANES, LANES)) * v
            acc[r % 4] = term if acc[r % 4] is None else acc[r % 4] + term
        obuf[tk] = (acc[0] + acc[1]) + (acc[2] + acc[3])

    def run_sub(sb, slot, next_idx, next_base):
        rows = pl.ds(pl.multiple_of(sb * sub, SUBLANES), sub)
        chunk = lambda j: slice(j * LANES, (j + 1) * LANES)
        hn = hn_ref[rows, :]
        for r, tile in enumerate(_swap_rows_and_chunks([hn[:, chunk(j)][None] for j in range(SUBLANES)])):
            hbuf[r] = tile[0]
        wait(slot)

        def body(tk, carry):
            if next_idx is not None:
                issue_token(next_idx, next_base, tk, 1 - slot)
            compute_token(sb, tk, slot)
            return carry

        lax.fori_loop(0, sub, body, 0)
        for j, part in enumerate(_swap_rows_and_chunks([obuf[r][None] for r in range(sub)])):
            o_ref[rows, chunk(j)] = x_ref[rows, chunk(j)] + part[0]

    @pl.when(step == 0)
    def _():
        lax.fori_loop(0, sub, lambda tk, c: (issue_token(idx_ref, 0, tk, 0), c)[1], 0)

    def pair(sp, carry):
        run_sub(2 * sp, 0, idx_ref, (2 * sp + 1) * sub * k)
        run_sub(2 * sp + 1, 1, idx_ref, (2 * sp + 2) * sub * k)
        return carry

    lax.fori_loop(0, n_sub // 2 - 1, pair, 0)
    run_sub(n_sub - 2, 0, idx_ref, (n_sub - 1) * sub * k)

    @pl.when(step < last_step)
    def _():
        run_sub(n_sub - 1, 1, idx_next_ref, 0)

    @pl.when(step == last_step)
    def _():
        run_sub(n_sub - 1, 1, None, 0)


def _peer_experts(x2, hn, idx_flat, gate_t, uv, nb):
    d = x2.shape[1]
    k = gate_t.shape[1]
    t = ROW_BLOCK
    per_front = gate_t.shape[2] // t
    n = nb * t
    sub = GATHER_SUB
    assert t % (2 * sub) == 0 and sub == SUBLANES and d == SUBLANES * LANES
    row = pl.BlockSpec((t, d), lambda i: (i, 0))
    return pl.pallas_call(
        functools.partial(_peer_experts_kernel, sub=sub),
        grid=(nb,),
        in_specs=[
            pl.BlockSpec((t * k,), lambda i: (i,), memory_space=pltpu.SMEM),
            pl.BlockSpec((t * k,), lambda i: (jnp.minimum(i + 1, nb - 1),), memory_space=pltpu.SMEM),
            row,
            pl.BlockSpec((1, k, t), lambda i: (i // per_front, 0, i % per_front)),
            row,
            pl.BlockSpec(memory_space=pl.ANY),
        ],
        out_specs=row,
        out_shape=jax.ShapeDtypeStruct((n, d), F32),
        scratch_shapes=[
            pltpu.VMEM((sub * k, SUBLANES, LANES), jnp.int32),
            pltpu.VMEM((sub * k, SUBLANES, LANES), jnp.int32),
            pltpu.VMEM((sub, k, LANES), F32),
            pltpu.VMEM((sub, SUBLANES, LANES), F32),
            pltpu.VMEM((sub, SUBLANES, LANES), F32),
            pltpu.SemaphoreType.DMA((2,)),
        ],
        compiler_params=_params(("arbitrary",)),
        name="peer_experts",
    )(idx_flat, idx_flat, hn, gate_t, x2, uv)


def _pack_kernel(u_ref, v_ref, o_ref):
    bits = lambda a: lax.bitcast_convert_type(a.astype(BF16).astype(F32), jnp.int32)
    words = (bits(v_ref[...]) & jnp.int32(-65536)) | lax.shift_right_logical(bits(u_ref[...]), 16)
    g = words.shape[0] // SUBLANES
    parts = [words[:, j * LANES:(j + 1) * LANES].reshape(g, SUBLANES, LANES) for j in range(SUBLANES)]
    for r, tile in enumerate(_swap_rows_and_chunks(parts)):
        o_ref[:, r] = tile


def _pack_experts(u, v):
    e, d = u.shape
    t = ROW_BLOCK
    assert d == SUBLANES * LANES and e % t == 0
    out = pl.pallas_call(
        _pack_kernel,
        grid=(e // t,),
        in_specs=[pl.BlockSpec((t, d), lambda i: (i, 0))] * 2,
        out_specs=pl.BlockSpec((t // SUBLANES, SUBLANES, SUBLANES, LANES), lambda i: (i, 0, 0, 0)),
        out_shape=jax.ShapeDtypeStruct((e // SUBLANES, SUBLANES, SUBLANES, LANES), jnp.int32),
        compiler_params=_params(("parallel",)),
        name="pack_experts",
    )(u, v)
    return out.reshape(e, SUBLANES, LANES)


def _pack_halves(t):
    half = t.shape[1] // 2
    bits = lambda a: lax.bitcast_convert_type(a.astype(BF16), jnp.uint16).astype(jnp.uint32)
    return lax.bitcast_convert_type((bits(t[:, half:]) << 16) | bits(t[:, :half]), jnp.int32)


def _sc_peer_experts(idx, hn, gate, x, u16, v16, first, n):
    k = idx.shape[1]
    d = hn.shape[1]
    half = d // 2
    ln = SC_LANES
    workers = SC_CORES * SC_SUBCORES
    nch = k // SC_CHUNK
    ntask = 2 * nch
    assert n % (workers * SC_TOKENS) == 0 and half % (ln * SC_ACC_WORDS) == 0 and SC_CHUNK % SC_ROW_GROUP == 0
    assert ntask % SC_SLOTS == 0 and SC_AHEAD < SC_SLOTS
    per_w = n // workers
    mesh = plsc.VectorSubcoreMesh(core_axis_name="c", subcore_axis_name="s")
    hi_mask = jnp.int32(-65536)

    def words(w):
        return lax.bitcast_convert_type(w << 16, F32), lax.bitcast_convert_type(w & hi_mask, F32)

    @functools.partial(
        pl.kernel, mesh=mesh, out_type=jax.ShapeDtypeStruct((n, d), F32),
        scratch_types=[
            pltpu.VMEM((SC_TOKENS, k), jnp.int32),
            pltpu.VMEM((SC_TOKENS, d), F32),
            pltpu.VMEM((SC_TOKENS, k), F32),
            pltpu.VMEM((SC_TOKENS, d), F32),
            [pltpu.VMEM((SC_CHUNK, half), jnp.int32)] * SC_SLOTS,
            pltpu.VMEM((k, ln), F32),
            pltpu.VMEM((k,), F32),
            [pltpu.SemaphoreType.DMA] * SC_SLOTS,
        ],
        compiler_params=pltpu.CompilerParams(needs_layout_passes=False, use_tc_tiling_on_sc=True),
        name="peer_experts_sc",
    )
    def kern(idx_hbm, hn_hbm, gate_hbm, x_hbm, u_hbm, v_hbm, o_hbm, idx_v, hn_v, gate_v, x_v, ring, part, w_v, sems):
        wid = lax.axis_index("s") * SC_CORES + lax.axis_index("c")
        lanes = lax.iota(jnp.int32, ln)

        def chunk_copy(ti, task):
            table = u_hbm if task < nch else v_hbm
            rows = idx_v.at[ti, pl.ds((task % nch) * SC_CHUNK, SC_CHUNK)]
            return pltpu.make_async_copy(table.at[rows], ring[task % SC_SLOTS], sems[task % SC_SLOTS])

        def down_chunk(ti, c, buf):
            @pl.loop(0, SC_CHUNK // SC_ROW_GROUP)
            def _(rg):
                r0 = rg * SC_ROW_GROUP
                lo_acc = [jnp.zeros((ln,), F32) for _ in range(SC_ROW_GROUP)]
                hi_acc = [jnp.zeros((ln,), F32) for _ in range(SC_ROW_GROUP)]
                for j in range(half // ln):
                    xl = hn_v[ti, pl.ds(j * ln, ln)]
                    xh = hn_v[ti, pl.ds(half + j * ln, ln)]
                    for i in range(SC_ROW_GROUP):
                        lo, hi = words(buf[r0 + i, pl.ds(j * ln, ln)])
                        lo_acc[i] = lo_acc[i] + lo * xl
                        hi_acc[i] = hi_acc[i] + hi * xh
                for i in range(SC_ROW_GROUP):
                    part[c * SC_CHUNK + r0 + i, :] = lo_acc[i] + hi_acc[i]

        def expert_weights(ti):
            @pl.loop(0, k // ln)
            def _(g):
                rows = g * ln + lanes
                a = jnp.zeros((ln,), F32)
                for l in range(ln):
                    a = a + plsc.load_gather(part, [rows, jnp.full((ln,), l, jnp.int32)])
                z = math.sqrt(2.0 / math.pi) * (a + 0.044715 * a * a * a)
                tanh = 1.0 - 2.0 / (jnp.exp(2.0 * z) + 1.0)
                w_v[pl.ds(g * ln, ln)] = gate_v[ti, pl.ds(g * ln, ln)] * (0.5 * a * (1.0 + tanh))

        def up_chunk(ti, c, buf):
            for jb in range(half // ln // SC_ACC_WORDS):
                cols = [(jb * SC_ACC_WORDS + jj) * ln for jj in range(SC_ACC_WORDS)]
                init = []
                for col in cols:
                    init += [x_v[ti, pl.ds(col, ln)], x_v[ti, pl.ds(half + col, ln)]]

                def body(r, accs, cols=cols):
                    ws = plsc.load_gather(w_v, [jnp.zeros((ln,), jnp.int32) + (c * SC_CHUNK + r)])
                    out = []
                    for jj, col in enumerate(cols):
                        lo, hi = words(buf[r, pl.ds(col, ln)])
                        out += [accs[2 * jj] + ws * lo, accs[2 * jj + 1] + ws * hi]
                    return tuple(out)

                accs = lax.fori_loop(0, SC_CHUNK, body, tuple(init))
                for jj, col in enumerate(cols):
                    x_v[ti, pl.ds(col, ln)] = accs[2 * jj]
                    x_v[ti, pl.ds(half + col, ln)] = accs[2 * jj + 1]

        @pl.loop(0, per_w // SC_TOKENS)
        def _(bi):
            out_base = wid * per_w + bi * SC_TOKENS
            base = first + out_base
            pltpu.sync_copy(idx_hbm.at[pl.ds(base, SC_TOKENS)], idx_v)
            pltpu.sync_copy(hn_hbm.at[pl.ds(base, SC_TOKENS)], hn_v)
            pltpu.sync_copy(gate_hbm.at[pl.ds(base, SC_TOKENS)], gate_v)
            pltpu.sync_copy(x_hbm.at[pl.ds(base, SC_TOKENS)], x_v)
            for task in range(SC_AHEAD):
                chunk_copy(0, task).start()

            @pl.loop(0, SC_TOKENS)
            def _(ti):
                for task in range(ntask):
                    chunk_copy(ti, task).wait()
                    ahead = task + SC_AHEAD
                    if ahead < ntask:
                        chunk_copy(ti, ahead).start()
                    else:
                        @pl.when(ti + 1 < SC_TOKENS)
                        def _(ahead=ahead):
                            chunk_copy(ti + 1, ahead - ntask).start()
                    buf = ring[task % SC_SLOTS]
                    if task < nch:
                        down_chunk(ti, task, buf)
                        if task == nch - 1:
                            expert_weights(ti)
                    else:
                        up_chunk(ti, task - nch, buf)

            pltpu.sync_copy(x_v, o_hbm.at[pl.ds(out_base, SC_TOKENS)])

    return kern(idx, hn, gate, x, u16, v16)


def _peer_layer(x2, g, w_q, keys, u, v, sc_share):
    hn, gate_t, idx_rows, gate_rows = _peer_front(x2, g, w_q, keys)
    n, k = idx_rows.shape
    t = ROW_BLOCK
    nb = n // t
    nb_sc = (nb * sc_share) // SC_SHARE_DEN
    n_tc = (nb - nb_sc) * t
    out_tc = _peer_experts(x2, hn, idx_rows.reshape(n * k), gate_t, _pack_experts(u, v), nb - nb_sc)
    if nb_sc == 0:
        return out_tc
    out_sc = _sc_peer_experts(idx_rows, hn, gate_rows, x2, _pack_halves(u), _pack_halves(v), n_tc, n - n_tc)
    return jnp.concatenate([out_tc, out_sc], axis=0)


def _ple_kernel(x_ref, p_ref, g_ref, wg_ref, wp_ref, fg_ref, o_ref, *, final):
    x = x_ref[...]
    gate = jax.nn.sigmoid(_dot(_rms(x, g_ref[...]), wg_ref[...]))
    y = x + gate * _dot(p_ref[...], wp_ref[...])
    o_ref[...] = _rms(y, fg_ref[...]) if final else y


def _ple_layer(x2, p2, g, w_g, w_p, final_g, final):
    n, d = x2.shape
    t = ROW_BLOCK
    row = lambda width: pl.BlockSpec((t, width), lambda i: (i, 0))
    return pl.pallas_call(
        functools.partial(_ple_kernel, final=final),
        grid=(n // t,),
        in_specs=[row(d), row(p2.shape[1]), _full((1, d)), _full(w_g.shape), _full(w_p.shape), _full((1, d))],
        out_specs=row(d),
        out_shape=jax.ShapeDtypeStruct((n, d), F32),
        compiler_params=_params(("parallel",)),
        name="ple",
    )(x2, p2, g.reshape(1, d), w_g.astype(BF16), w_p.astype(BF16), final_g.reshape(1, d))


def kernel(x, p, positions, mix_norm, ffn_norm, pool_w, pool_scale, kv_in_norm, w_dkv, kv_norm, w_ukv, w_dq, q_norm, w_uq, w_o, peer_wq, peer_keys, peer_u, peer_v, ple_norm, ple_wg, ple_wp, final_norm):
    weights = (mix_norm, ffn_norm, pool_w, pool_scale, kv_in_norm, w_dkv, kv_norm, w_ukv, w_dq, q_norm, w_uq, w_o,
               peer_wq, peer_keys, peer_u, peer_v, ple_norm, ple_wg, ple_wp, final_norm)
    per = x.shape[0] // BATCH_CHAINS
    assert per * BATCH_CHAINS == x.shape[0]
    outs = [_trunk(x[c * per:(c + 1) * per], p[:, c * per:(c + 1) * per], positions[c * per:(c + 1) * per], *weights)
            for c in range(BATCH_CHAINS)]
    return jnp.concatenate(outs, axis=0)


def _trunk(x, p, positions, mix_norm, ffn_norm, pool_w, pool_scale, kv_in_norm, w_dkv, kv_norm, w_ukv, w_dq, q_norm, w_uq, w_o, peer_wq, peer_keys, peer_u, peer_v, ple_norm, ple_wg, ple_wp, final_norm):
    b, s, d = x.shape
    depth = p.shape[0]
    assert len(SC_SHARE_NUM) == depth
    n_a = pool_w.shape[0]
    x2 = x.reshape(b * s, d)
    cos, sin = _rope_tables(positions)
    k_sh = v_sh = None
    for i in range(depth):
        if i < n_a:
            x2 = _pool_layer(x2, s, mix_norm[i], pool_w[i], pool_scale[i])
        else:
            if i == n_a:
                k_sh, v_sh = _shared_kv(x2, cos, sin, kv_in_norm, w_dkv, kv_norm, w_ukv)
            j = i - n_a
            q = _q_proj(x2, cos, sin, mix_norm[i], w_dq[j], q_norm[j], w_uq[j])
            o = _flash_attention(q, k_sh, v_sh, b, s)
            x2 = _out_proj(x2, o, w_o[j])
        x2 = _peer_layer(x2, ffn_norm[i], peer_wq[i], peer_keys[i], peer_u[i], peer_v[i], SC_SHARE_NUM[i])
        x2 = _ple_layer(x2, p[i].reshape(b * s, -1), ple_norm[i], ple_wg[i], ple_wp[i], final_norm, i == depth - 1)
    return x2.reshape(b, s, d)
```

```python
import functools
import math

import jax
import jax.numpy as jnp
from jax import lax
from jax.experimental import pallas as pl
from jax.experimental.pallas import tpu as pltpu
from jax.experimental.pallas import tpu_sc as plsc

F32 = jnp.float32
BF16 = jnp.bfloat16
EPS = 1e-6

POOL_WINDOWS = (2, 4, 8, 16)
POOL_HALO = 16
MLA_HEADS = 8
QK_NOPE = 128
QK_ROPE = 64
V_HEAD = 128
KV_RANK = 256
ROPE_THETA = 10000.0
ATTN_SCALE = 1.0 / math.sqrt(QK_NOPE + QK_ROPE)
PEER_HEADS = 8
N_KEYS = 128
PEER_TOPK = 16
PEER_DHALF = 128

SUBLANES = 8
LANES = 128
ROW_BLOCK = 256
PEER_BLOCK = 512
POOL_BLOCK = 512
ATTN_BLOCK = 512
ATTN_DTYPE = jnp.bfloat16
GATHER_SUB = 8
VMEM_LIMIT = 56 * 1024 * 1024
SC_CORES = 2
SC_SUBCORES = 16
SC_LANES = 16
SC_TOKENS = 16
SC_CHUNK = 32
SC_SLOTS = 4
SC_AHEAD = 3
SC_ROW_GROUP = 4
SC_ACC_WORDS = 8
SC_SHARE_NUM, SC_SHARE_DEN = (48, 48, 60, 52), 64
BATCH_CHAINS = 4


def _params(sem):
    return pltpu.CompilerParams(dimension_semantics=sem, vmem_limit_bytes=VMEM_LIMIT)


def _rms(x, g):
    ms = jnp.mean(x * x, axis=-1, keepdims=True)
    return x * lax.rsqrt(ms + EPS) * g


def _dot(a, w):
    return jnp.dot(a.astype(w.dtype), w, preferred_element_type=F32)


def _split(a):
    hi = a.astype(BF16)
    return hi, (a - hi.astype(F32)).astype(BF16)


def _dot3(a, b, dims):
    mm = lambda x, y: lax.dot_general(x, y, (dims, ((), ())), preferred_element_type=F32)
    return mm(a[0], b[0]) + (mm(a[0], b[1]) + mm(a[1], b[0]))


def _full(shape):
    return pl.BlockSpec(shape, lambda *_: (0,) * len(shape))


def _pool_kernel(x_ref, halo_ref, g_ref, w_ref, sc_ref, o_ref, *, ts, blocks_per_seq):
    blk = pl.program_id(0) % blocks_per_seq
    x = x_ref[...]
    g = g_ref[...]
    h = _rms(x, g)
    hh = jnp.where(blk == 0, 0.0, _rms(halo_ref[...], g))
    full = jnp.concatenate([hh, h], axis=0)
    t = blk * ts + lax.broadcasted_iota(jnp.int32, (ts, 1), 0)
    gw = w_ref.shape[-1]
    outs = []
    for gi, w in enumerate(POOL_WINDOWS):
        p = full[:, gi * gw:(gi + 1) * gw]
        step = 1
        while step < w:
            p = p + pltpu.roll(p, step, axis=0)
            step *= 2
        win = p[POOL_HALO:, :]
        cnt = jnp.minimum(t + 1, w).astype(F32)
        pooled = win / cnt - h[:, gi * gw:(gi + 1) * gw]
        outs.append(_dot(pooled, w_ref[gi]))
    o_ref[...] = x + jnp.concatenate(outs, axis=-1) * sc_ref[...]


def _pool_layer(x2, seq, g, w, scale):
    n, d = x2.shape
    ts = min(POOL_BLOCK, seq)
    assert seq % ts == 0 and ts % POOL_HALO == 0 and all(wd & (wd - 1) == 0 and wd <= POOL_HALO for wd in POOL_WINDOWS)
    hb = ts // POOL_HALO
    return pl.pallas_call(
        functools.partial(_pool_kernel, ts=ts, blocks_per_seq=seq // ts),
        grid=(n // ts,),
        in_specs=[
            pl.BlockSpec((ts, d), lambda i: (i, 0)),
            pl.BlockSpec((POOL_HALO, d), lambda i: (jnp.maximum(i * hb - 1, 0), 0)),
            _full((1, d)),
            _full(w.shape),
            _full((1, d)),
        ],
        out_specs=pl.BlockSpec((ts, d), lambda i: (i, 0)),
        out_shape=jax.ShapeDtypeStruct((n, d), F32),
        compiler_params=_params(("parallel",)),
        name="pool_mixer",
    )(x2, x2, g.reshape(1, d), w.astype(BF16), scale.reshape(1, d))


def _rope_kernel(pos_ref, inv_ref, cos_ref, sin_ref):
    ang = inv_ref[...] * pos_ref[0].astype(F32)
    cos_ref[0] = jnp.cos(ang)
    sin_ref[0] = jnp.sin(ang)


def _rope_tables(positions):
    b, s = positions.shape
    half = QK_ROPE // 2
    inv = ROPE_THETA ** (-jnp.arange(0, QK_ROPE, 2, dtype=F32) / QK_ROPE)
    cos_t, sin_t = pl.pallas_call(
        _rope_kernel,
        grid=(b,),
        in_specs=[pl.BlockSpec((1, 1, s), lambda i: (i, 0, 0)), _full((half, 1))],
        out_specs=[pl.BlockSpec((1, half, s), lambda i: (i, 0, 0))] * 2,
        out_shape=[jax.ShapeDtypeStruct((b, half, s), F32)] * 2,
        compiler_params=_params(("parallel",)),
        name="rope_tables",
    )(positions.reshape(b, 1, s), inv.reshape(half, 1))
    to_rows = lambda a: a.transpose(0, 2, 1).reshape(b * s, half)
    return to_rows(cos_t), to_rows(sin_t)


def _kv_kernel(x_ref, cos_ref, sin_ref, gin_ref, wdkv_ref, gkv_ref, wukv_ref, k_ref, v_ref):
    h = _rms(x_ref[...], gin_ref[...])
    ckv = _dot(h, wdkv_ref[...])
    c = _rms(ckv[:, :KV_RANK], gkv_ref[...])
    half = QK_ROPE // 2
    x1 = ckv[:, KV_RANK:KV_RANK + half]
    x2 = ckv[:, KV_RANK + half:]
    cos = cos_ref[...]
    sin = sin_ref[...]
    k1 = x1 * cos - x2 * sin
    k2 = x2 * cos + x1 * sin
    kv = _dot(c, wukv_ref[...])
    for hd in range(MLA_HEADS):
        k_ref[hd] = jnp.concatenate([kv[:, hd * QK_NOPE:(hd + 1) * QK_NOPE], k1, k2], axis=-1).astype(k_ref.dtype)
        off = MLA_HEADS * QK_NOPE + hd * V_HEAD
        v_ref[hd] = kv[:, off:off + V_HEAD].astype(v_ref.dtype)


def _split_heads(w, bounds):
    w3 = w.reshape(w.shape[0], MLA_HEADS, -1)
    return jnp.concatenate([w3[:, :, lo:hi].reshape(w.shape[0], -1) for lo, hi in bounds], axis=1)


def _shared_kv(x2, cos, sin, kv_in_norm, w_dkv, kv_norm, w_ukv):
    n, d = x2.shape
    t = ROW_BLOCK
    half = QK_ROPE // 2
    w_ukv_p = _split_heads(w_ukv, [(0, QK_NOPE), (QK_NOPE, QK_NOPE + V_HEAD)])
    row = lambda width: pl.BlockSpec((t, width), lambda i: (i, 0))
    return pl.pallas_call(
        _kv_kernel,
        grid=(n // t,),
        in_specs=[row(d), row(half), row(half), _full((1, d)), _full(w_dkv.shape), _full((1, KV_RANK)),
                  _full(w_ukv_p.shape)],
        out_specs=[pl.BlockSpec((MLA_HEADS, t, QK_NOPE + QK_ROPE), lambda i: (0, i, 0)),
                   pl.BlockSpec((MLA_HEADS, t, V_HEAD), lambda i: (0, i, 0))],
        out_shape=[jax.ShapeDtypeStruct((MLA_HEADS, n, QK_NOPE + QK_ROPE), ATTN_DTYPE),
                   jax.ShapeDtypeStruct((MLA_HEADS, n, V_HEAD), ATTN_DTYPE)],
        compiler_params=_params(("parallel",)),
        name="shared_kv",
    )(x2, cos, sin, kv_in_norm.reshape(1, d), w_dkv.astype(BF16), kv_norm.reshape(1, KV_RANK), w_ukv_p.astype(BF16))


def _q_kernel(x_ref, cos_ref, sin_ref, g_ref, wdq_ref, gq_ref, wuq_ref, q_ref):
    h = _rms(x_ref[...], g_ref[...])
    cq = _rms(_dot(h, wdq_ref[...]), gq_ref[...])
    q = _dot(cq, wuq_ref[...]) * ATTN_SCALE
    half = QK_ROPE // 2
    nope = MLA_HEADS * QK_NOPE
    cos = jnp.concatenate([cos_ref[...]] * MLA_HEADS, axis=-1)
    sin = jnp.concatenate([sin_ref[...]] * MLA_HEADS, axis=-1)
    p1 = q[:, nope:nope + MLA_HEADS * half]
    p2 = q[:, nope + MLA_HEADS * half:]
    r1 = p1 * cos - p2 * sin
    r2 = p2 * cos + p1 * sin
    for hd in range(MLA_HEADS):
        q_ref[hd] = jnp.concatenate([q[:, hd * QK_NOPE:(hd + 1) * QK_NOPE], r1[:, hd * half:(hd + 1) * half],
                                     r2[:, hd * half:(hd + 1) * half]], axis=-1).astype(q_ref.dtype)


def _q_proj(x2, cos, sin, g, w_dq, q_norm, w_uq):
    n, d = x2.shape
    t = ROW_BLOCK
    half = QK_ROPE // 2
    rank = w_dq.shape[1]
    w_uq_p = _split_heads(w_uq, [(0, QK_NOPE), (QK_NOPE, QK_NOPE + half), (QK_NOPE + half, QK_NOPE + QK_ROPE)])
    row = lambda width: pl.BlockSpec((t, width), lambda i: (i, 0))
    return pl.pallas_call(
        _q_kernel,
        grid=(n // t,),
        in_specs=[row(d), row(half), row(half), _full((1, d)), _full(w_dq.shape), _full((1, rank)),
                  _full(w_uq_p.shape)],
        out_specs=pl.BlockSpec((MLA_HEADS, t, QK_NOPE + QK_ROPE), lambda i: (0, i, 0)),
        out_shape=jax.ShapeDtypeStruct((MLA_HEADS, n, QK_NOPE + QK_ROPE), ATTN_DTYPE),
        compiler_params=_params(("parallel",)),
        name="mla_q",
    )(x2, cos, sin, g.reshape(1, d), w_dq.astype(BF16), q_norm.reshape(1, rank), w_uq_p.astype(BF16))


def _flash_kernel(q_ref, k_ref, v_ref, o_ref, m_sc, l_sc, acc_sc, *, tb):
    qi = pl.program_id(2)
    ki = pl.program_id(3)

    @pl.when(ki == 0)
    def _():
        m_sc[...] = jnp.full_like(m_sc, -jnp.inf)
        l_sc[...] = jnp.zeros_like(l_sc)
        acc_sc[...] = jnp.zeros_like(acc_sc)

    def update(masked):
        s = lax.dot_general(q_ref[0], k_ref[0], (((1,), (1,)), ((), ())), preferred_element_type=F32)
        if masked:
            row = lax.broadcasted_iota(jnp.int32, (tb, tb), 0)
            col = lax.broadcasted_iota(jnp.int32, (tb, tb), 1)
            s = jnp.where(col <= row, s, -jnp.inf)
        m_prev = m_sc[...]
        m_new = jnp.maximum(m_prev, jnp.max(s, axis=-1, keepdims=True))
        p = jnp.exp(s - m_new)
        alpha = jnp.exp(m_prev - m_new)
        l_sc[...] = alpha * l_sc[...] + jnp.sum(p, axis=-1, keepdims=True)
        acc_sc[...] = alpha * acc_sc[...] + jnp.dot(p.astype(v_ref.dtype), v_ref[0], preferred_element_type=F32)
        m_sc[...] = m_new

    pl.when(ki < qi)(functools.partial(update, False))
    pl.when(ki == qi)(functools.partial(update, True))

    @pl.when(ki == qi)
    def _():
        o_ref[...] = (acc_sc[...] / l_sc[...]).astype(o_ref.dtype)


def _flash_attention(q, k, v, batch, seq):
    tb = min(ATTN_BLOCK, seq)
    nb = seq // tb
    n = batch * seq
    dk = q.shape[-1]
    return pl.pallas_call(
        functools.partial(_flash_kernel, tb=tb),
        grid=(batch, MLA_HEADS, nb, nb),
        in_specs=[
            pl.BlockSpec((1, tb, dk), lambda b, h, qi, ki: (h, b * nb + qi, 0)),
            pl.BlockSpec((1, tb, dk), lambda b, h, qi, ki: (h, b * nb + jnp.minimum(ki, qi), 0)),
            pl.BlockSpec((1, tb, V_HEAD), lambda b, h, qi, ki: (h, b * nb + jnp.minimum(ki, qi), 0)),
        ],
        out_specs=pl.BlockSpec((tb, V_HEAD), lambda b, h, qi, ki: (b * nb + qi, h)),
        out_shape=jax.ShapeDtypeStruct((n, MLA_HEADS * V_HEAD), ATTN_DTYPE),
        scratch_shapes=[pltpu.VMEM((tb, 1), F32), pltpu.VMEM((tb, 1), F32), pltpu.VMEM((tb, V_HEAD), F32)],
        compiler_params=_params(("parallel", "parallel", "parallel", "arbitrary")),
        name="mla_flash",
    )(q, k, v)


def _oproj_kernel(x_ref, o_ref, w_ref, y_ref):
    y_ref[...] = x_ref[...] + _dot(o_ref[...], w_ref[...])


def _out_proj(x2, o, w_o):
    n, d = x2.shape
    t = ROW_BLOCK
    row = lambda width: pl.BlockSpec((t, width), lambda i: (i, 0))
    return pl.pallas_call(
        _oproj_kernel,
        grid=(n // t,),
        in_specs=[row(d), row(o.shape[1]), _full(w_o.shape)],
        out_specs=row(d),
        out_shape=jax.ShapeDtypeStruct((n, d), F32),
        compiler_params=_params(("parallel",)),
        name="mla_out",
    )(x2, o, w_o.astype(BF16))


def _top_rows(problems, k):
    def pick(s, payload, r, vals, idxs):
        rows, t = s.shape
        iota = lax.broadcasted_iota(jnp.int32, (rows, t), 0)
        out_row = lax.broadcasted_iota(jnp.int32, (k, t), 0)
        m = jnp.max(s, axis=0, keepdims=True)
        pos = jnp.min(jnp.where(s == m, iota, rows), axis=0, keepdims=True)
        hit = iota == pos
        got = pos if payload is None else jnp.sum(jnp.where(hit, payload, 0), axis=0, keepdims=True)
        return (jnp.where(hit, -jnp.inf, s), jnp.where(out_row == r, m, vals), jnp.where(out_row == r, got, idxs))

    def body(r, carry):
        return tuple(pick(s, payload, r, vals, idxs) for (s, vals, idxs), (_, payload) in zip(carry, problems))

    t = problems[0][0].shape[1]
    init = tuple((s, jnp.zeros((k, t), F32), jnp.zeros((k, t), jnp.int32)) for s, _ in problems)
    return [(vals, idxs) for _, vals, idxs in lax.fori_loop(0, k, body, init)]


def _pair_candidates(s1, i1, s2, i2):
    neg = jnp.float32(-jnp.inf)
    sub = lax.broadcasted_iota(jnp.int32, (SUBLANES, s1.shape[1]), 0)
    cs, ci = [s1[0:1] + s2], [i1[0:1] * N_KEYS + i2]
    for a in range(1, PEER_TOPK // 2):
        keep = sub < PEER_TOPK // (a + 1)
        cs.append(jnp.where(keep, s1[a:a + 1] + s2[:SUBLANES], neg))
        ci.append(i1[a:a + 1] * N_KEYS + i2[:SUBLANES])
    cs.append(s1[PEER_TOPK // 2:] + s2[0:1])
    ci.append(i1[PEER_TOPK // 2:] * N_KEYS + i2[0:1])
    return jnp.concatenate(cs, axis=0), jnp.concatenate(ci, axis=0)


def _peer_front_kernel(x_ref, g_ref, wq_hi_ref, wq_lo_ref, keys_hi_ref, keys_lo_ref, hn_ref, gate_ref, idx_rows_ref,
                       gate_rows_ref, idx_sc, gate_sc, *, lane_chunk):
    head = pl.program_id(1)
    head_rows = pl.ds(pl.multiple_of(head * PEER_TOPK, PEER_TOPK), PEER_TOPK)
    hn = _rms(x_ref[...], g_ref[...])

    @pl.when(head == 0)
    def _():
        hn_ref[...] = hn

    q = _dot3(_split(hn), (wq_hi_ref[...], wq_lo_ref[...]), ((1,), (0,)))
    cands = []
    for c in range(hn.shape[0] // lane_chunk):
        tok = slice(c * lane_chunk, (c + 1) * lane_chunk)
        scores = [_dot3((keys_hi_ref[0, half], keys_lo_ref[0, half]),
                        _split(q[tok, half * PEER_DHALF:(half + 1) * PEER_DHALF]), ((1,), (1,))) for half in range(2)]
        (s1, i1), (s2, i2) = _top_rows([(sc, None) for sc in scores], PEER_TOPK)
        cands.append(_pair_candidates(s1, i1, s2, i2))
    best = []
    for c in range(0, len(cands), 2):
        best += _top_rows(cands[c:c + 2], PEER_TOPK)
    for c, (best_s, best_i) in enumerate(best):
        tok = slice(c * lane_chunk, (c + 1) * lane_chunk)
        e = jnp.exp(best_s - best_s[0:1, :])
        gate = e / jnp.sum(e, axis=0, keepdims=True)
        gate_ref[0, :, tok] = gate
        gate_sc[head_rows, tok] = gate
        idx_sc[head_rows, tok] = best_i

    @pl.when(head == pl.num_programs(1) - 1)
    def _():
        idx_rows_ref[...] = idx_sc[...].T
        gate_rows_ref[...] = gate_sc[...].T


def _peer_front(x2, g, w_q, keys):
    n, d = x2.shape
    t = PEER_BLOCK
    nb = n // t
    dq = 2 * PEER_DHALF
    k = PEER_HEADS * PEER_TOPK
    return pl.pallas_call(
        functools.partial(_peer_front_kernel, lane_chunk=128),
        grid=(nb, PEER_HEADS),
        in_specs=[
            pl.BlockSpec((t, d), lambda i, h: (i, 0)),
            _full((1, d)),
            pl.BlockSpec((d, dq), lambda i, h: (0, h)),
            pl.BlockSpec((d, dq), lambda i, h: (0, h)),
            pl.BlockSpec((1, 2, N_KEYS, PEER_DHALF), lambda i, h: (h, 0, 0, 0)),
            pl.BlockSpec((1, 2, N_KEYS, PEER_DHALF), lambda i, h: (h, 0, 0, 0)),
        ],
        out_specs=[
            pl.BlockSpec((t, d), lambda i, h: (i, 0)),
            pl.BlockSpec((1, PEER_TOPK, t), lambda i, h: (i, h, 0)),
            pl.BlockSpec((t, k), lambda i, h: (i, 0)),
            pl.BlockSpec((t, k), lambda i, h: (i, 0)),
        ],
        out_shape=[
            jax.ShapeDtypeStruct((n, d), F32),
            jax.ShapeDtypeStruct((nb, k, t), F32),
            jax.ShapeDtypeStruct((n, k), jnp.int32),
            jax.ShapeDtypeStruct((n, k), F32),
        ],
        scratch_shapes=[pltpu.VMEM((k, t), jnp.int32), pltpu.VMEM((k, t), F32)],
        compiler_params=_params(("parallel", "arbitrary")),
        name="peer_front",
    )(x2, g.reshape(1, d), *_split(w_q), *_split(keys))


def _row_sums(p):
    k = p.shape[0]
    p = p.reshape(k // SUBLANES, SUBLANES, SUBLANES, LANES)
    s = lax.broadcasted_iota(jnp.int32, (1, 1, SUBLANES, LANES), 2)
    a, b = p[:, :4], p[:, 4:]
    m = (s & 4) == 0
    p = jnp.where(m, a, b) + pltpu.roll(jnp.where(m, b, a), 4, axis=2)
    for dist in (2, 1):
        h = p.shape[1] // 2
        a, b = p[:, :h], p[:, h:]
        m = (s & dist) == 0
        p = jnp.where(m, a + pltpu.roll(a, SUBLANES - dist, axis=2), b + pltpu.roll(b, dist, axis=2))
    return p.reshape(k, LANES)


def _swap_rows_and_chunks(parts):
    parts = list(parts)
    q = lax.broadcasted_iota(jnp.int32, parts[0].shape, 1)
    for d in (4, 2, 1):
        low = (q & d) == 0
        for p0 in range(SUBLANES):
            if p0 & d:
                continue
            a, b = parts[p0], parts[p0 + d]
            parts[p0] = jnp.where(low, a, pltpu.roll(b, d, axis=1))
            parts[p0 + d] = jnp.where(low, pltpu.roll(a, SUBLANES - d, axis=1), b)
    return parts


def _peer_experts_kernel(idx_ref, idx_next_ref, hn_ref, gate_ref, x_ref, uv_hbm, o_ref, buf_a, buf_b, wbuf, hbuf, obuf, sem,
                         *, sub):
    step = pl.program_id(0)
    last_step = pl.num_programs(0) - 1
    t = hn_ref.shape[0]
    k = gate_ref.shape[1]
    n_sub = t // sub
    bufs = (buf_a, buf_b)
    lane = lax.broadcasted_iota(jnp.int32, (1, t), 1)
    hi_mask = jnp.int32(-65536)

    def issue_token(idx, base, tk, slot):
        for r in range(k):
            e = idx[base + tk * k + r]
            pltpu.make_async_copy(uv_hbm.at[e], bufs[slot].at[tk * k + r], sem.at[slot]).start(priority=r % 2)

    def wait(slot):
        pltpu.make_async_copy(uv_hbm.at[pl.ds(0, sub * k)], bufs[slot], sem.at[slot]).wait()

    def compute_token(sb, tk, slot):
        buf = bufs[slot]
        tok = sb * sub + tk
        u = lax.bitcast_convert_type(buf[pl.ds(tk * k, k)] << 16, F32)
        a = jnp.sum(_row_sums(u * hbuf[tk][None]), axis=-1, keepdims=True)
        gcol = jnp.sum(jnp.where(lane == tok, gate_ref[0], 0.0), axis=-1, keepdims=True)
        wbuf[tk] = jnp.broadcast_to(gcol * jax.nn.gelu(a), (k, LANES))
        acc = [None] * 4
        for r in range(k):
            v = lax.bitcast_convert_type(buf[tk * k + r] & hi_mask, F32)
            term = jnp.broadcast_to(wbuf[tk, r:r + 1, :], (SUBLANES, LANES)) * v
            acc[r % 4] = term if acc[r % 4] is None else acc[r % 4] + term
        obuf[tk] = (acc[0] + acc[1]) + (acc[2] + acc[3])

    def run_sub(sb, slot, next_idx, next_base):
        rows = pl.ds(pl.multiple_of(sb * sub, SUBLANES), sub)
        chunk = lambda j: slice(j * LANES, (j + 1) * LANES)
        hn = hn_ref[rows, :]
        for r, tile in enumerate(_swap_rows_and_chunks([hn[:, chunk(j)][None] for j in range(SUBLANES)])):
            hbuf[r] = tile[0]
        wait(slot)

        def body(tk, carry):
            if next_idx is not None:
                issue_token(next_idx, next_base, tk, 1 - slot)
            compute_token(sb, tk, slot)
            return carry

        lax.fori_loop(0, sub, body, 0)
        for j, part in enumerate(_swap_rows_and_chunks([obuf[r][None] for r in range(sub)])):
            o_ref[rows, chunk(j)] = x_ref[rows, chunk(j)] + part[0]

    @pl.when(step == 0)
    def _():
        lax.fori_loop(0, sub, lambda tk, c: (issue_token(idx_ref, 0, tk, 0), c)[1], 0)

    def pair(sp, carry):
        run_sub(2 * sp, 0, idx_ref, (2 * sp + 1) * sub * k)
        run_sub(2 * sp + 1, 1, idx_ref, (2 * sp + 2) * sub * k)
        return carry

    lax.fori_loop(0, n_sub // 2 - 1, pair, 0)
    run_sub(n_sub - 2, 0, idx_ref, (n_sub - 1) * sub * k)

    @pl.when(step < last_step)
    def _():
        run_sub(n_sub - 1, 1, idx_next_ref, 0)

    @pl.when(step == last_step)
    def _():
        run_sub(n_sub - 1, 1, None, 0)


def _peer_experts(x2, hn, idx_flat, gate_t, uv, nb):
    d = x2.shape[1]
    k = gate_t.shape[1]
    t = ROW_BLOCK
    per_front = gate_t.shape[2] // t
    n = nb * t
    sub = GATHER_SUB
    assert t % (2 * sub) == 0 and sub == SUBLANES and d == SUBLANES * LANES
    row = pl.BlockSpec((t, d), lambda i: (i, 0))
    return pl.pallas_call(
        functools.partial(_peer_experts_kernel, sub=sub),
        grid=(nb,),
        in_specs=[
            pl.BlockSpec((t * k,), lambda i: (i,), memory_space=pltpu.SMEM),
            pl.BlockSpec((t * k,), lambda i: (jnp.minimum(i + 1, nb - 1),), memory_space=pltpu.SMEM),
            row,
            pl.BlockSpec((1, k, t), lambda i: (i // per_front, 0, i % per_front)),
            row,
            pl.BlockSpec(memory_space=pl.ANY),
        ],
        out_specs=row,
        out_shape=jax.ShapeDtypeStruct((n, d), F32),
        scratch_shapes=[
            pltpu.VMEM((sub * k, SUBLANES, LANES), jnp.int32),
            pltpu.VMEM((sub * k, SUBLANES, LANES), jnp.int32),
            pltpu.VMEM((sub, k, LANES), F32),
            pltpu.VMEM((sub, SUBLANES, LANES), F32),
            pltpu.VMEM((sub, SUBLANES, LANES), F32),
            pltpu.SemaphoreType.DMA((2,)),
        ],
        compiler_params=_params(("arbitrary",)),
        name="peer_experts",
    )(idx_flat, idx_flat, hn, gate_t, x2, uv)


def _pack_kernel(u_ref, v_ref, o_ref):
    bits = lambda a: lax.bitcast_convert_type(a.astype(BF16).astype(F32), jnp.int32)
    words = (bits(v_ref[...]) & jnp.int32(-65536)) | lax.shift_right_logical(bits(u_ref[...]), 16)
    g = words.shape[0] // SUBLANES
    parts = [words[:, j * LANES:(j + 1) * LANES].reshape(g, SUBLANES, LANES) for j in range(SUBLANES)]
    for r, tile in enumerate(_swap_rows_and_chunks(parts)):
        o_ref[:, r] = tile


def _pack_experts(u, v):
    e, d = u.shape
    t = ROW_BLOCK
    assert d == SUBLANES * LANES and e % t == 0
    out = pl.pallas_call(
        _pack_kernel,
        grid=(e // t,),
        in_specs=[pl.BlockSpec((t, d), lambda i: (i, 0))] * 2,
        out_specs=pl.BlockSpec((t // SUBLANES, SUBLANES, SUBLANES, LANES), lambda i: (i, 0, 0, 0)),
        out_shape=jax.ShapeDtypeStruct((e // SUBLANES, SUBLANES, SUBLANES, LANES), jnp.int32),
        compiler_params=_params(("parallel",)),
        name="pack_experts",
    )(u, v)
    return out.reshape(e, SUBLANES, LANES)


def _pack_halves(t):
    half = t.shape[1] // 2
    bits = lambda a: lax.bitcast_convert_type(a.astype(BF16), jnp.uint16).astype(jnp.uint32)
    return lax.bitcast_convert_type((bits(t[:, half:]) << 16) | bits(t[:, :half]), jnp.int32)


def _sc_peer_experts(idx, hn, gate, x, u16, v16, first, n):
    k = idx.shape[1]
    d = hn.shape[1]
    half = d // 2
    ln = SC_LANES
    workers = SC_CORES * SC_SUBCORES
    nch = k // SC_CHUNK
    ntask = 2 * nch
    assert n % (workers * SC_TOKENS) == 0 and half % (ln * SC_ACC_WORDS) == 0 and SC_CHUNK % SC_ROW_GROUP == 0
    assert ntask % SC_SLOTS == 0 and SC_AHEAD < SC_SLOTS
    per_w = n // workers
    mesh = plsc.VectorSubcoreMesh(core_axis_name="c", subcore_axis_name="s")
    hi_mask = jnp.int32(-65536)

    def words(w):
        return lax.bitcast_convert_type(w << 16, F32), lax.bitcast_convert_type(w & hi_mask, F32)

    @functools.partial(
        pl.kernel, mesh=mesh, out_type=jax.ShapeDtypeStruct((n, d), F32),
        scratch_types=[
            pltpu.VMEM((SC_TOKENS, k), jnp.int32),
            pltpu.VMEM((SC_TOKENS, d), F32),
            pltpu.VMEM((SC_TOKENS, k), F32),
            pltpu.VMEM((SC_TOKENS, d), F32),
            [pltpu.VMEM((SC_CHUNK, half), jnp.int32)] * SC_SLOTS,
            pltpu.VMEM((k, ln), F32),
            pltpu.VMEM((k,), F32),
            [pltpu.SemaphoreType.DMA] * SC_SLOTS,
        ],
        compiler_params=pltpu.CompilerParams(needs_layout_passes=False, use_tc_tiling_on_sc=True),
        name="peer_experts_sc",
    )
    def kern(idx_hbm, hn_hbm, gate_hbm, x_hbm, u_hbm, v_hbm, o_hbm, idx_v, hn_v, gate_v, x_v, ring, part, w_v, sems):
        wid = lax.axis_index("s") * SC_CORES + lax.axis_index("c")
        lanes = lax.iota(jnp.int32, ln)

        def chunk_copy(ti, task):
            table = u_hbm if task < nch else v_hbm
            rows = idx_v.at[ti, pl.ds((task % nch) * SC_CHUNK, SC_CHUNK)]
            return pltpu.make_async_copy(table.at[rows], ring[task % SC_SLOTS], sems[task % SC_SLOTS])

        def down_chunk(ti, c, buf):
            @pl.loop(0, SC_CHUNK // SC_ROW_GROUP)
            def _(rg):
                r0 = rg * SC_ROW_GROUP
                lo_acc = [jnp.zeros((ln,), F32) for _ in range(SC_ROW_GROUP)]
                hi_acc = [jnp.zeros((ln,), F32) for _ in range(SC_ROW_GROUP)]
                for j in range(half // ln):
                    xl = hn_v[ti, pl.ds(j * ln, ln)]
                    xh = hn_v[ti, pl.ds(half + j * ln, ln)]
                    for i in range(SC_ROW_GROUP):
                        lo, hi = words(buf[r0 + i, pl.ds(j * ln, ln)])
                        lo_acc[i] = lo_acc[i] + lo * xl
                        hi_acc[i] = hi_acc[i] + hi * xh
                for i in range(SC_ROW_GROUP):
                    part[c * SC_CHUNK + r0 + i, :] = lo_acc[i] + hi_acc[i]

        def expert_weights(ti):
            @pl.loop(0, k // ln)
            def _(g):
                rows = g * ln + lanes
                a = jnp.zeros((ln,), F32)
                for l in range(ln):
                    a = a + plsc.load_gather(part, [rows, jnp.full((ln,), l, jnp.int32)])
                z = math.sqrt(2.0 / math.pi) * (a + 0.044715 * a * a * a)
                tanh = 1.0 - 2.0 / (jnp.exp(2.0 * z) + 1.0)
                w_v[pl.ds(g * ln, ln)] = gate_v[ti, pl.ds(g * ln, ln)] * (0.5 * a * (1.0 + tanh))

        def up_chunk(ti, c, buf):
            for jb in range(half // ln // SC_ACC_WORDS):
                cols = [(jb * SC_ACC_WORDS + jj) * ln for jj in range(SC_ACC_WORDS)]
                init = []
                for col in cols:
                    init += [x_v[ti, pl.ds(col, ln)], x_v[ti, pl.ds(half + col, ln)]]

                def body(r, accs, cols=cols):
                    ws = plsc.load_gather(w_v, [jnp.zeros((ln,), jnp.int32) + (c * SC_CHUNK + r)])
                    out = []
                    for jj, col in enumerate(cols):
                        lo, hi = words(buf[r, pl.ds(col, ln)])
                        out += [accs[2 * jj] + ws * lo, accs[2 * jj + 1] + ws * hi]
                    return tuple(out)

                accs = lax.fori_loop(0, SC_CHUNK, body, tuple(init))
                for jj, col in enumerate(cols):
                    x_v[ti, pl.ds(col, ln)] = accs[2 * jj]
                    x_v[ti, pl.ds(half + col, ln)] = accs[2 * jj + 1]

        @pl.loop(0, per_w // SC_TOKENS)
        def _(bi):
            out_base = wid * per_w + bi * SC_TOKENS
            base = first + out_base
            pltpu.sync_copy(idx_hbm.at[pl.ds(base, SC_TOKENS)], idx_v)
            pltpu.sync_copy(hn_hbm.at[pl.ds(base, SC_TOKENS)], hn_v)
            pltpu.sync_copy(gate_hbm.at[pl.ds(base, SC_TOKENS)], gate_v)
            pltpu.sync_copy(x_hbm.at[pl.ds(base, SC_TOKENS)], x_v)
            for task in range(SC_AHEAD):
                chunk_copy(0, task).start()

            @pl.loop(0, SC_TOKENS)
            def _(ti):
                for task in range(ntask):
                    chunk_copy(ti, task).wait()
                    ahead = task + SC_AHEAD
                    if ahead < ntask:
                        chunk_copy(ti, ahead).start()
                    else:
                        @pl.when(ti + 1 < SC_TOKENS)
                        def _(ahead=ahead):
                            chunk_copy(ti + 1, ahead - ntask).start()
                    buf = ring[task % SC_SLOTS]
                    if task < nch:
                        down_chunk(ti, task, buf)
                        if task == nch - 1:
                            expert_weights(ti)
                    else:
                        up_chunk(ti, task - nch, buf)

            pltpu.sync_copy(x_v, o_hbm.at[pl.ds(out_base, SC_TOKENS)])

    return kern(idx, hn, gate, x, u16, v16)


def _peer_layer(x2, g, w_q, keys, u, v, sc_share):
    hn, gate_t, idx_rows, gate_rows = _peer_front(x2, g, w_q, keys)
    n, k = idx_rows.shape
    t = ROW_BLOCK
    nb = n // t
    nb_sc = (nb * sc_share) // SC_SHARE_DEN
    n_tc = (nb - nb_sc) * t
    out_tc = _peer_experts(x2, hn, idx_rows.reshape(n * k), gate_t, _pack_experts(u, v), nb - nb_sc)
    if nb_sc == 0:
        return out_tc
    out_sc = _sc_peer_experts(idx_rows, hn, gate_rows, x2, _pack_halves(u), _pack_halves(v), n_tc, n - n_tc)
    return jnp.concatenate([out_tc, out_sc], axis=0)


def _ple_kernel(x_ref, p_ref, g_ref, wg_ref, wp_ref, fg_ref, o_ref, *, final):
    x = x_ref[...]
    gate = jax.nn.sigmoid(_dot(_rms(x, g_ref[...]), wg_ref[...]))
    y = x + gate * _dot(p_ref[...], wp_ref[...])
    o_ref[...] = _rms(y, fg_ref[...]) if final else y


def _ple_layer(x2, p2, g, w_g, w_p, final_g, final):
    n, d = x2.shape
    t = ROW_BLOCK
    row = lambda width: pl.BlockSpec((t, width), lambda i: (i, 0))
    return pl.pallas_call(
        functools.partial(_ple_kernel, final=final),
        grid=(n // t,),
        in_specs=[row(d), row(p2.shape[1]), _full((1, d)), _full(w_g.shape), _full(w_p.shape), _full((1, d))],
        out_specs=row(d),
        out_shape=jax.ShapeDtypeStruct((n, d), F32),
        compiler_params=_params(("parallel",)),
        name="ple",
    )(x2, p2, g.reshape(1, d), w_g.astype(BF16), w_p.astype(BF16), final_g.reshape(1, d))


def kernel(x, p, positions, mix_norm, ffn_norm, pool_w, pool_scale, kv_in_norm, w_dkv, kv_norm, w_ukv, w_dq, q_norm, w_uq, w_o, peer_wq, peer_keys, peer_u, peer_v, ple_norm, ple_wg, ple_wp, final_norm):
    weights = (mix_norm, ffn_norm, pool_w, pool_scale, kv_in_norm, w_dkv, kv_norm, w_ukv, w_dq, q_norm, w_uq, w_o,
               peer_wq, peer_keys, peer_u, peer_v, ple_norm, ple_wg, ple_wp, final_norm)
    per = x.shape[0] // BATCH_CHAINS
    assert per * BATCH_CHAINS == x.shape[0]
    outs = [_trunk(x[c * per:(c + 1) * per], p[:, c * per:(c + 1) * per], positions[c * per:(c + 1) * per], *weights)
            for c in range(BATCH_CHAINS)]
    return jnp.concatenate(outs, axis=0)


def _trunk(x, p, positions, mix_norm, ffn_norm, pool_w, pool_scale, kv_in_norm, w_dkv, kv_norm, w_ukv, w_dq, q_norm, w_uq, w_o, peer_wq, peer_keys, peer_u, peer_v, ple_norm, ple_wg, ple_wp, final_norm):
    b, s, d = x.shape
    depth = p.shape[0]
    assert len(SC_SHARE_NUM) == depth
    n_a = pool_w.shape[0]
    x2 = x.reshape(b * s, d)
    cos, sin = _rope_tables(positions)
    k_sh = v_sh = None
    for i in range(depth):
        if i < n_a:
            x2 = _pool_layer(x2, s, mix_norm[i], pool_w[i], pool_scale[i])
        else:
            if i == n_a:
                k_sh, v_sh = _shared_kv(x2, cos, sin, kv_in_norm, w_dkv, kv_norm, w_ukv)
            j = i - n_a
            q = _q_proj(x2, cos, sin, mix_norm[i], w_dq[j], q_norm[j], w_uq[j])
            o = _flash_attention(q, k_sh, v_sh, b, s)
            x2 = _out_proj(x2, o, w_o[j])
        x2 = _peer_layer(x2, ffn_norm[i], peer_wq[i], peer_keys[i], peer_u[i], peer_v[i], SC_SHARE_NUM[i])
        x2 = _ple_layer(x2, p[i].reshape(b * s, -1), ple_norm[i], ple_wg[i], ple_wp[i], final_norm, i == depth - 1)
    return x2.reshape(b, s, d)
```

```python
import functools
import math

import jax
import jax.numpy as jnp
from jax import lax
from jax.experimental import pallas as pl
from jax.experimental.pallas import tpu as pltpu
from jax.experimental.pallas import tpu_sc as plsc

F32 = jnp.float32
BF16 = jnp.bfloat16
EPS = 1e-6

POOL_WINDOWS = (2, 4, 8, 16)
POOL_HALO = 16
MLA_HEADS = 8
QK_NOPE = 128
QK_ROPE = 64
V_HEAD = 128
KV_RANK = 256
ROPE_THETA = 10000.0
ATTN_SCALE = 1.0 / math.sqrt(QK_NOPE + QK_ROPE)
PEER_HEADS = 8
N_KEYS = 128
PEER_TOPK = 16
PEER_DHALF = 128

SUBLANES = 8
LANES = 128
ROW_BLOCK = 256
PEER_BLOCK = 512
POOL_BLOCK = 512
ATTN_BLOCK = 512
ATTN_DTYPE = jnp.bfloat16
GATHER_SUB = 8
VMEM_LIMIT = 56 * 1024 * 1024
SC_CORES = 2
SC_SUBCORES = 16
SC_LANES = 16
SC_TOKENS = 16
SC_CHUNK = 32
SC_SLOTS = 4
SC_AHEAD = 3
SC_ROW_GROUP = 4
SC_ACC_WORDS = 8
SC_SHARE_NUM, SC_SHARE_DEN = (48, 48, 56, 56), 64
BATCH_CHAINS = 8


def _params(sem):
    return pltpu.CompilerParams(dimension_semantics=sem, vmem_limit_bytes=VMEM_LIMIT)


def _rms(x, g):
    ms = jnp.mean(x * x, axis=-1, keepdims=True)
    return x * lax.rsqrt(ms + EPS) * g


def _dot(a, w):
    return jnp.dot(a.astype(w.dtype), w, preferred_element_type=F32)


def _split(a):
    hi = a.astype(BF16)
    return hi, (a - hi.astype(F32)).astype(BF16)


def _dot3(a, b, dims):
    mm = lambda x, y: lax.dot_general(x, y, (dims, ((), ())), preferred_element_type=F32)
    return mm(a[0], b[0]) + (mm(a[0], b[1]) + mm(a[1], b[0]))


def _full(shape):
    return pl.BlockSpec(shape, lambda *_: (0,) * len(shape))


def _pool_kernel(x_ref, halo_ref, g_ref, w_ref, sc_ref, o_ref, *, ts, blocks_per_seq):
    blk = pl.program_id(0) % blocks_per_seq
    x = x_ref[...]
    g = g_ref[...]
    h = _rms(x, g)
    hh = jnp.where(blk == 0, 0.0, _rms(halo_ref[...], g))
    full = jnp.concatenate([hh, h], axis=0)
    t = blk * ts + lax.broadcasted_iota(jnp.int32, (ts, 1), 0)
    gw = w_ref.shape[-1]
    outs = []
    for gi, w in enumerate(POOL_WINDOWS):
        p = full[:, gi * gw:(gi + 1) * gw]
        step = 1
        while step < w:
            p = p + pltpu.roll(p, step, axis=0)
            step *= 2
        win = p[POOL_HALO:, :]
        cnt = jnp.minimum(t + 1, w).astype(F32)
        pooled = win / cnt - h[:, gi * gw:(gi + 1) * gw]
        outs.append(_dot(pooled, w_ref[gi]))
    o_ref[...] = x + jnp.concatenate(outs, axis=-1) * sc_ref[...]


def _pool_layer(x2, seq, g, w, scale):
    n, d = x2.shape
    ts = min(POOL_BLOCK, seq)
    assert seq % ts == 0 and ts % POOL_HALO == 0 and all(wd & (wd - 1) == 0 and wd <= POOL_HALO for wd in POOL_WINDOWS)
    hb = ts // POOL_HALO
    return pl.pallas_call(
        functools.partial(_pool_kernel, ts=ts, blocks_per_seq=seq // ts),
        grid=(n // ts,),
        in_specs=[
            pl.BlockSpec((ts, d), lambda i: (i, 0)),
            pl.BlockSpec((POOL_HALO, d), lambda i: (jnp.maximum(i * hb - 1, 0), 0)),
            _full((1, d)),
            _full(w.shape),
            _full((1, d)),
        ],
        out_specs=pl.BlockSpec((ts, d), lambda i: (i, 0)),
        out_shape=jax.ShapeDtypeStruct((n, d), F32),
        compiler_params=_params(("parallel",)),
        name="pool_mixer",
    )(x2, x2, g.reshape(1, d), w.astype(BF16), scale.reshape(1, d))


def _rope_kernel(pos_ref, inv_ref, cos_ref, sin_ref):
    ang = inv_ref[...] * pos_ref[0].astype(F32)
    cos_ref[0] = jnp.cos(ang)
    sin_ref[0] = jnp.sin(ang)


def _rope_tables(positions):
    b, s = positions.shape
    half = QK_ROPE // 2
    inv = ROPE_THETA ** (-jnp.arange(0, QK_ROPE, 2, dtype=F32) / QK_ROPE)
    cos_t, sin_t = pl.pallas_call(
        _rope_kernel,
        grid=(b,),
        in_specs=[pl.BlockSpec((1, 1, s), lambda i: (i, 0, 0)), _full((half, 1))],
        out_specs=[pl.BlockSpec((1, half, s), lambda i: (i, 0, 0))] * 2,
        out_shape=[jax.ShapeDtypeStruct((b, half, s), F32)] * 2,
        compiler_params=_params(("parallel",)),
        name="rope_tables",
    )(positions.reshape(b, 1, s), inv.reshape(half, 1))
    to_rows = lambda a: a.transpose(0, 2, 1).reshape(b * s, half)
    return to_rows(cos_t), to_rows(sin_t)


def _kv_kernel(x_ref, cos_ref, sin_ref, gin_ref, wdkv_ref, gkv_ref, wukv_ref, k_ref, v_ref):
    h = _rms(x_ref[...], gin_ref[...])
    ckv = _dot(h, wdkv_ref[...])
    c = _rms(ckv[:, :KV_RANK], gkv_ref[...])
    half = QK_ROPE // 2
    x1 = ckv[:, KV_RANK:KV_RANK + half]
    x2 = ckv[:, KV_RANK + half:]
    cos = cos_ref[...]
    sin = sin_ref[...]
    k1 = x1 * cos - x2 * sin
    k2 = x2 * cos + x1 * sin
    kv = _dot(c, wukv_ref[...])
    for hd in range(MLA_HEADS):
        k_ref[hd] = jnp.concatenate([kv[:, hd * QK_NOPE:(hd + 1) * QK_NOPE], k1, k2], axis=-1).astype(k_ref.dtype)
        off = MLA_HEADS * QK_NOPE + hd * V_HEAD
        v_ref[hd] = kv[:, off:off + V_HEAD].astype(v_ref.dtype)


def _split_heads(w, bounds):
    w3 = w.reshape(w.shape[0], MLA_HEADS, -1)
    return jnp.concatenate([w3[:, :, lo:hi].reshape(w.shape[0], -1) for lo, hi in bounds], axis=1)


def _shared_kv(x2, cos, sin, kv_in_norm, w_dkv, kv_norm, w_ukv):
    n, d = x2.shape
    t = ROW_BLOCK
    half = QK_ROPE // 2
    w_ukv_p = _split_heads(w_ukv, [(0, QK_NOPE), (QK_NOPE, QK_NOPE + V_HEAD)])
    row = lambda width: pl.BlockSpec((t, width), lambda i: (i, 0))
    return pl.pallas_call(
        _kv_kernel,
        grid=(n // t,),
        in_specs=[row(d), row(half), row(half), _full((1, d)), _full(w_dkv.shape), _full((1, KV_RANK)),
                  _full(w_ukv_p.shape)],
        out_specs=[pl.BlockSpec((MLA_HEADS, t, QK_NOPE + QK_ROPE), lambda i: (0, i, 0)),
                   pl.BlockSpec((MLA_HEADS, t, V_HEAD), lambda i: (0, i, 0))],
        out_shape=[jax.ShapeDtypeStruct((MLA_HEADS, n, QK_NOPE + QK_ROPE), ATTN_DTYPE),
                   jax.ShapeDtypeStruct((MLA_HEADS, n, V_HEAD), ATTN_DTYPE)],
        compiler_params=_params(("parallel",)),
        name="shared_kv",
    )(x2, cos, sin, kv_in_norm.reshape(1, d), w_dkv.astype(BF16), kv_norm.reshape(1, KV_RANK), w_ukv_p.astype(BF16))


def _q_kernel(x_ref, cos_ref, sin_ref, g_ref, wdq_ref, gq_ref, wuq_ref, q_ref):
    h = _rms(x_ref[...], g_ref[...])
    cq = _rms(_dot(h, wdq_ref[...]), gq_ref[...])
    q = _dot(cq, wuq_ref[...]) * ATTN_SCALE
    half = QK_ROPE // 2
    nope = MLA_HEADS * QK_NOPE
    cos = jnp.concatenate([cos_ref[...]] * MLA_HEADS, axis=-1)
    sin = jnp.concatenate([sin_ref[...]] * MLA_HEADS, axis=-1)
    p1 = q[:, nope:nope + MLA_HEADS * half]
    p2 = q[:, nope + MLA_HEADS * half:]
    r1 = p1 * cos - p2 * sin
    r2 = p2 * cos + p1 * sin
    for hd in range(MLA_HEADS):
        q_ref[hd] = jnp.concatenate([q[:, hd * QK_NOPE:(hd + 1) * QK_NOPE], r1[:, hd * half:(hd + 1) * half],
                                     r2[:, hd * half:(hd + 1) * half]], axis=-1).astype(q_ref.dtype)


def _q_proj(x2, cos, sin, g, w_dq, q_norm, w_uq):
    n, d = x2.shape
    t = ROW_BLOCK
    half = QK_ROPE // 2
    rank = w_dq.shape[1]
    w_uq_p = _split_heads(w_uq, [(0, QK_NOPE), (QK_NOPE, QK_NOPE + half), (QK_NOPE + half, QK_NOPE + QK_ROPE)])
    row = lambda width: pl.BlockSpec((t, width), lambda i: (i, 0))
    return pl.pallas_call(
        _q_kernel,
        grid=(n // t,),
        in_specs=[row(d), row(half), row(half), _full((1, d)), _full(w_dq.shape), _full((1, rank)),
                  _full(w_uq_p.shape)],
        out_specs=pl.BlockSpec((MLA_HEADS, t, QK_NOPE + QK_ROPE), lambda i: (0, i, 0)),
        out_shape=jax.ShapeDtypeStruct((MLA_HEADS, n, QK_NOPE + QK_ROPE), ATTN_DTYPE),
        compiler_params=_params(("parallel",)),
        name="mla_q",
    )(x2, cos, sin, g.reshape(1, d), w_dq.astype(BF16), q_norm.reshape(1, rank), w_uq_p.astype(BF16))


def _flash_kernel(q_ref, k_ref, v_ref, o_ref, m_sc, l_sc, acc_sc, *, tb):
    qi = pl.program_id(2)
    ki = pl.program_id(3)

    @pl.when(ki == 0)
    def _():
        m_sc[...] = jnp.full_like(m_sc, -jnp.inf)
        l_sc[...] = jnp.zeros_like(l_sc)
        acc_sc[...] = jnp.zeros_like(acc_sc)

    def update(masked):
        s = lax.dot_general(q_ref[0], k_ref[0], (((1,), (1,)), ((), ())), preferred_element_type=F32)
        if masked:
            row = lax.broadcasted_iota(jnp.int32, (tb, tb), 0)
            col = lax.broadcasted_iota(jnp.int32, (tb, tb), 1)
            s = jnp.where(col <= row, s, -jnp.inf)
        m_prev = m_sc[...]
        m_new = jnp.maximum(m_prev, jnp.max(s, axis=-1, keepdims=True))
        p = jnp.exp(s - m_new)
        alpha = jnp.exp(m_prev - m_new)
        l_sc[...] = alpha * l_sc[...] + jnp.sum(p, axis=-1, keepdims=True)
        acc_sc[...] = alpha * acc_sc[...] + jnp.dot(p.astype(v_ref.dtype), v_ref[0], preferred_element_type=F32)
        m_sc[...] = m_new

    pl.when(ki < qi)(functools.partial(update, False))
    pl.when(ki == qi)(functools.partial(update, True))

    @pl.when(ki == qi)
    def _():
        o_ref[...] = (acc_sc[...] / l_sc[...]).astype(o_ref.dtype)


def _flash_attention(q, k, v, batch, seq):
    tb = min(ATTN_BLOCK, seq)
    nb = seq // tb
    n = batch * seq
    dk = q.shape[-1]
    return pl.pallas_call(
        functools.partial(_flash_kernel, tb=tb),
        grid=(batch, MLA_HEADS, nb, nb),
        in_specs=[
            pl.BlockSpec((1, tb, dk), lambda b, h, qi, ki: (h, b * nb + qi, 0)),
            pl.BlockSpec((1, tb, dk), lambda b, h, qi, ki: (h, b * nb + jnp.minimum(ki, qi), 0)),
            pl.BlockSpec((1, tb, V_HEAD), lambda b, h, qi, ki: (h, b * nb + jnp.minimum(ki, qi), 0)),
        ],
        out_specs=pl.BlockSpec((tb, V_HEAD), lambda b, h, qi, ki: (b * nb + qi, h)),
        out_shape=jax.ShapeDtypeStruct((n, MLA_HEADS * V_HEAD), ATTN_DTYPE),
        scratch_shapes=[pltpu.VMEM((tb, 1), F32), pltpu.VMEM((tb, 1), F32), pltpu.VMEM((tb, V_HEAD), F32)],
        compiler_params=_params(("parallel", "parallel", "parallel", "arbitrary")),
        name="mla_flash",
    )(q, k, v)


def _oproj_kernel(x_ref, o_ref, w_ref, y_ref):
    y_ref[...] = x_ref[...] + _dot(o_ref[...], w_ref[...])


def _out_proj(x2, o, w_o):
    n, d = x2.shape
    t = ROW_BLOCK
    row = lambda width: pl.BlockSpec((t, width), lambda i: (i, 0))
    return pl.pallas_call(
        _oproj_kernel,
        grid=(n // t,),
        in_specs=[row(d), row(o.shape[1]), _full(w_o.shape)],
        out_specs=row(d),
        out_shape=jax.ShapeDtypeStruct((n, d), F32),
        compiler_params=_params(("parallel",)),
        name="mla_out",
    )(x2, o, w_o.astype(BF16))


def _top_rows(problems, k):
    def pick(s, payload, r, vals, idxs):
        rows, t = s.shape
        iota = lax.broadcasted_iota(jnp.int32, (rows, t), 0)
        out_row = lax.broadcasted_iota(jnp.int32, (k, t), 0)
        m = jnp.max(s, axis=0, keepdims=True)
        pos = jnp.min(jnp.where(s == m, iota, rows), axis=0, keepdims=True)
        hit = iota == pos
        got = pos if payload is None else jnp.sum(jnp.where(hit, payload, 0), axis=0, keepdims=True)
        return (jnp.where(hit, -jnp.inf, s), jnp.where(out_row == r, m, vals), jnp.where(out_row == r, got, idxs))

    def body(r, carry):
        return tuple(pick(s, payload, r, vals, idxs) for (s, vals, idxs), (_, payload) in zip(carry, problems))

    t = problems[0][0].shape[1]
    init = tuple((s, jnp.zeros((k, t), F32), jnp.zeros((k, t), jnp.int32)) for s, _ in problems)
    return [(vals, idxs) for _, vals, idxs in lax.fori_loop(0, k, body, init)]


def _pair_candidates(s1, i1, s2, i2):
    neg = jnp.float32(-jnp.inf)
    sub = lax.broadcasted_iota(jnp.int32, (SUBLANES, s1.shape[1]), 0)
    cs, ci = [s1[0:1] + s2], [i1[0:1] * N_KEYS + i2]
    for a in range(1, PEER_TOPK // 2):
        keep = sub < PEER_TOPK // (a + 1)
        cs.append(jnp.where(keep, s1[a:a + 1] + s2[:SUBLANES], neg))
        ci.append(i1[a:a + 1] * N_KEYS + i2[:SUBLANES])
    cs.append(s1[PEER_TOPK // 2:] + s2[0:1])
    ci.append(i1[PEER_TOPK // 2:] * N_KEYS + i2[0:1])
    return jnp.concatenate(cs, axis=0), jnp.concatenate(ci, axis=0)


def _peer_front_kernel(x_ref, g_ref, wq_hi_ref, wq_lo_ref, keys_hi_ref, keys_lo_ref, hn_ref, gate_ref, idx_rows_ref,
                       gate_rows_ref, idx_sc, gate_sc, *, lane_chunk):
    head = pl.program_id(1)
    head_rows = pl.ds(pl.multiple_of(head * PEER_TOPK, PEER_TOPK), PEER_TOPK)
    hn = _rms(x_ref[...], g_ref[...])

    @pl.when(head == 0)
    def _():
        hn_ref[...] = hn

    q = _dot3(_split(hn), (wq_hi_ref[...], wq_lo_ref[...]), ((1,), (0,)))
    cands = []
    for c in range(hn.shape[0] // lane_chunk):
        tok = slice(c * lane_chunk, (c + 1) * lane_chunk)
        scores = [_dot3((keys_hi_ref[0, half], keys_lo_ref[0, half]),
                        _split(q[tok, half * PEER_DHALF:(half + 1) * PEER_DHALF]), ((1,), (1,))) for half in range(2)]
        (s1, i1), (s2, i2) = _top_rows([(sc, None) for sc in scores], PEER_TOPK)
        cands.append(_pair_candidates(s1, i1, s2, i2))
    best = []
    for c in range(0, len(cands), 2):
        best += _top_rows(cands[c:c + 2], PEER_TOPK)
    for c, (best_s, best_i) in enumerate(best):
        tok = slice(c * lane_chunk, (c + 1) * lane_chunk)
        e = jnp.exp(best_s - best_s[0:1, :])
        gate = e / jnp.sum(e, axis=0, keepdims=True)
        gate_ref[0, :, tok] = gate
        gate_sc[head_rows, tok] = gate
        idx_sc[head_rows, tok] = best_i

    @pl.when(head == pl.num_programs(1) - 1)
    def _():
        idx_rows_ref[...] = idx_sc[...].T
        gate_rows_ref[...] = gate_sc[...].T


def _peer_front(x2, g, w_q, keys):
    n, d = x2.shape
    t = PEER_BLOCK
    nb = n // t
    dq = 2 * PEER_DHALF
    k = PEER_HEADS * PEER_TOPK
    return pl.pallas_call(
        functools.partial(_peer_front_kernel, lane_chunk=128),
        grid=(nb, PEER_HEADS),
        in_specs=[
            pl.BlockSpec((t, d), lambda i, h: (i, 0)),
            _full((1, d)),
            pl.BlockSpec((d, dq), lambda i, h: (0, h)),
            pl.BlockSpec((d, dq), lambda i, h: (0, h)),
            pl.BlockSpec((1, 2, N_KEYS, PEER_DHALF), lambda i, h: (h, 0, 0, 0)),
            pl.BlockSpec((1, 2, N_KEYS, PEER_DHALF), lambda i, h: (h, 0, 0, 0)),
        ],
        out_specs=[
            pl.BlockSpec((t, d), lambda i, h: (i, 0)),
            pl.BlockSpec((1, PEER_TOPK, t), lambda i, h: (i, h, 0)),
            pl.BlockSpec((t, k), lambda i, h: (i, 0)),
            pl.BlockSpec((t, k), lambda i, h: (i, 0)),
        ],
        out_shape=[
            jax.ShapeDtypeStruct((n, d), F32),
            jax.ShapeDtypeStruct((nb, k, t), F32),
            jax.ShapeDtypeStruct((n, k), jnp.int32),
            jax.ShapeDtypeStruct((n, k), F32),
        ],
        scratch_shapes=[pltpu.VMEM((k, t), jnp.int32), pltpu.VMEM((k, t), F32)],
        compiler_params=_params(("parallel", "arbitrary")),
        name="peer_front",
    )(x2, g.reshape(1, d), *_split(w_q), *_split(keys))


def _row_sums(p):
    k = p.shape[0]
    p = p.reshape(k // SUBLANES, SUBLANES, SUBLANES, LANES)
    s = lax.broadcasted_iota(jnp.int32, (1, 1, SUBLANES, LANES), 2)
    a, b = p[:, :4], p[:, 4:]
    m = (s & 4) == 0
    p = jnp.where(m, a, b) + pltpu.roll(jnp.where(m, b, a), 4, axis=2)
    for dist in (2, 1):
        h = p.shape[1] // 2
        a, b = p[:, :h], p[:, h:]
        m = (s & dist) == 0
        p = jnp.where(m, a + pltpu.roll(a, SUBLANES - dist, axis=2), b + pltpu.roll(b, dist, axis=2))
    return p.reshape(k, LANES)


def _swap_rows_and_chunks(parts):
    parts = list(parts)
    q = lax.broadcasted_iota(jnp.int32, parts[0].shape, 1)
    for d in (4, 2, 1):
        low = (q & d) == 0
        for p0 in range(SUBLANES):
            if p0 & d:
                continue
            a, b = parts[p0], parts[p0 + d]
            parts[p0] = jnp.where(low, a, pltpu.roll(b, d, axis=1))
            parts[p0 + d] = jnp.where(low, pltpu.roll(a, SUBLANES - d, axis=1), b)
    return parts


def _peer_experts_kernel(idx_ref, idx_next_ref, hn_ref, gate_ref, x_ref, uv_hbm, o_ref, buf_a, buf_b, wbuf, hbuf, obuf, sem,
                         *, sub):
    step = pl.program_id(0)
    last_step = pl.num_programs(0) - 1
    t = hn_ref.shape[0]
    k = gate_ref.shape[1]
    n_sub = t // sub
    bufs = (buf_a, buf_b)
    lane = lax.broadcasted_iota(jnp.int32, (1, t), 1)
    hi_mask = jnp.int32(-65536)

    def issue_token(idx, base, tk, slot):
        for r in range(k):
            e = idx[base + tk * k + r]
            pltpu.make_async_copy(uv_hbm.at[e], bufs[slot].at[tk * k + r], sem.at[slot]).start(priority=r % 2)

    def wait(slot):
        pltpu.make_async_copy(uv_hbm.at[pl.ds(0, sub * k)], bufs[slot], sem.at[slot]).wait()

    def compute_token(sb, tk, slot):
        buf = bufs[slot]
        tok = sb * sub + tk
        u = lax.bitcast_convert_type(buf[pl.ds(tk * k, k)] << 16, F32)
        a = jnp.sum(_row_sums(u * hbuf[tk][None]), axis=-1, keepdims=True)
        gcol = jnp.sum(jnp.where(lane == tok, gate_ref[0], 0.0), axis=-1, keepdims=True)
        wbuf[tk] = jnp.broadcast_to(gcol * jax.nn.gelu(a), (k, LANES))
        acc = [None] * 4
        for r in range(k):
            v = lax.bitcast_convert_type(buf[tk * k + r] & hi_mask, F32)
            term = jnp.broadcast_to(wbuf[tk, r:r + 1, :], (SUBLANES, LANES)) * v
            acc[r % 4] = term if acc[r % 4] is None else acc[r % 4] + term
        obuf[tk] = (acc[0] + acc[1]) + (acc[2] + acc[3])

    def run_sub(sb, slot, next_idx, next_base):
        rows = pl.ds(pl.multiple_of(sb * sub, SUBLANES), sub)
        chunk = lambda j: slice(j * LANES, (j + 1) * LANES)
        hn = hn_ref[rows, :]
        for r, tile in enumerate(_swap_rows_and_chunks([hn[:, chunk(j)][None] for j in range(SUBLANES)])):
            hbuf[r] = tile[0]
        wait(slot)

        def body(tk, carry):
            if next_idx is not None:
                issue_token(next_idx, next_base, tk, 1 - slot)
            compute_token(sb, tk, slot)
            return carry

        lax.fori_loop(0, sub, body, 0)
        for j, part in enumerate(_swap_rows_and_chunks([obuf[r][None] for r in range(sub)])):
            o_ref[rows, chunk(j)] = x_ref[rows, chunk(j)] + part[0]

    @pl.when(step == 0)
    def _():
        lax.fori_loop(0, sub, lambda tk, c: (issue_token(idx_ref, 0, tk, 0), c)[1], 0)

    def pair(sp, carry):
        run_sub(2 * sp, 0, idx_ref, (2 * sp + 1) * sub * k)
        run_sub(2 * sp + 1, 1, idx_ref, (2 * sp + 2) * sub * k)
        return carry

    lax.fori_loop(0, n_sub // 2 - 1, pair, 0)
    run_sub(n_sub - 2, 0, idx_ref, (n_sub - 1) * sub * k)

    @pl.when(step < last_step)
    def _():
        run_sub(n_sub - 1, 1, idx_next_ref, 0)

    @pl.when(step == last_step)
    def _():
        run_sub(n_sub - 1, 1, None, 0)


def _peer_experts(x2, hn, idx_flat, gate_t, uv, nb):
    d = x2.shape[1]
    k = gate_t.shape[1]
    t = ROW_BLOCK
    per_front = gate_t.shape[2] // t
    n = nb * t
    sub = GATHER_SUB
    assert t % (2 * sub) == 0 and sub == SUBLANES and d == SUBLANES * LANES
    row = pl.BlockSpec((t, d), lambda i: (i, 0))
    return pl.pallas_call(
        functools.partial(_peer_experts_kernel, sub=sub),
        grid=(nb,),
        in_specs=[
            pl.BlockSpec((t * k,), lambda i: (i,), memory_space=pltpu.SMEM),
            pl.BlockSpec((t * k,), lambda i: (jnp.minimum(i + 1, nb - 1),), memory_space=pltpu.SMEM),
            row,
            pl.BlockSpec((1, k, t), lambda i: (i // per_front, 0, i % per_front)),
            row,
            pl.BlockSpec(memory_space=pl.ANY),
        ],
        out_specs=row,
        out_shape=jax.ShapeDtypeStruct((n, d), F32),
        scratch_shapes=[
            pltpu.VMEM((sub * k, SUBLANES, LANES), jnp.int32),
            pltpu.VMEM((sub * k, SUBLANES, LANES), jnp.int32),
            pltpu.VMEM((sub, k, LANES), F32),
            pltpu.VMEM((sub, SUBLANES, LANES), F32),
            pltpu.VMEM((sub, SUBLANES, LANES), F32),
            pltpu.SemaphoreType.DMA((2,)),
        ],
        compiler_params=_params(("arbitrary",)),
        name="peer_experts",
    )(idx_flat, idx_flat, hn, gate_t, x2, uv)


def _pack_kernel(u_ref, v_ref, o_ref):
    bits = lambda a: lax.bitcast_convert_type(a.astype(BF16).astype(F32), jnp.int32)
    words = (bits(v_ref[...]) & jnp.int32(-65536)) | lax.shift_right_logical(bits(u_ref[...]), 16)
    g = words.shape[0] // SUBLANES
    parts = [words[:, j * LANES:(j + 1) * LANES].reshape(g, SUBLANES, LANES) for j in range(SUBLANES)]
    for r, tile in enumerate(_swap_rows_and_chunks(parts)):
        o_ref[:, r] = tile


def _pack_experts(u, v):
    e, d = u.shape
    t = ROW_BLOCK
    assert d == SUBLANES * LANES and e % t == 0
    out = pl.pallas_call(
        _pack_kernel,
        grid=(e // t,),
        in_specs=[pl.BlockSpec((t, d), lambda i: (i, 0))] * 2,
        out_specs=pl.BlockSpec((t // SUBLANES, SUBLANES, SUBLANES, LANES), lambda i: (i, 0, 0, 0)),
        out_shape=jax.ShapeDtypeStruct((e // SUBLANES, SUBLANES, SUBLANES, LANES), jnp.int32),
        compiler_params=_params(("parallel",)),
        name="pack_experts",
    )(u, v)
    return out.reshape(e, SUBLANES, LANES)


def _pack_halves(t):
    half = t.shape[1] // 2
    bits = lambda a: lax.bitcast_convert_type(a.astype(BF16), jnp.uint16).astype(jnp.uint32)
    return lax.bitcast_convert_type((bits(t[:, half:]) << 16) | bits(t[:, :half]), jnp.int32)


def _sc_peer_experts(idx, hn, gate, x, u16, v16, first, n):
    k = idx.shape[1]
    d = hn.shape[1]
    half = d // 2
    ln = SC_LANES
    workers = SC_CORES * SC_SUBCORES
    nch = k // SC_CHUNK
    ntask = 2 * nch
    assert n % (workers * SC_TOKENS) == 0 and half % (ln * SC_ACC_WORDS) == 0 and SC_CHUNK % SC_ROW_GROUP == 0
    assert ntask % SC_SLOTS == 0 and SC_AHEAD < SC_SLOTS
    per_w = n // workers
    mesh = plsc.VectorSubcoreMesh(core_axis_name="c", subcore_axis_name="s")
    hi_mask = jnp.int32(-65536)

    def words(w):
        return lax.bitcast_convert_type(w << 16, F32), lax.bitcast_convert_type(w & hi_mask, F32)

    @functools.partial(
        pl.kernel, mesh=mesh, out_type=jax.ShapeDtypeStruct((n, d), F32),
        scratch_types=[
            pltpu.VMEM((SC_TOKENS, k), jnp.int32),
            pltpu.VMEM((SC_TOKENS, d), F32),
            pltpu.VMEM((SC_TOKENS, k), F32),
            pltpu.VMEM((SC_TOKENS, d), F32),
            [pltpu.VMEM((SC_CHUNK, half), jnp.int32)] * SC_SLOTS,
            pltpu.VMEM((k, ln), F32),
            pltpu.VMEM((k,), F32),
            [pltpu.SemaphoreType.DMA] * SC_SLOTS,
        ],
        compiler_params=pltpu.CompilerParams(needs_layout_passes=False, use_tc_tiling_on_sc=True),
        name="peer_experts_sc",
    )
    def kern(idx_hbm, hn_hbm, gate_hbm, x_hbm, u_hbm, v_hbm, o_hbm, idx_v, hn_v, gate_v, x_v, ring, part, w_v, sems):
        wid = lax.axis_index("s") * SC_CORES + lax.axis_index("c")
        lanes = lax.iota(jnp.int32, ln)

        def chunk_copy(ti, task):
            table = u_hbm if task < nch else v_hbm
            rows = idx_v.at[ti, pl.ds((task % nch) * SC_CHUNK, SC_CHUNK)]
            return pltpu.make_async_copy(table.at[rows], ring[task % SC_SLOTS], sems[task % SC_SLOTS])

        def down_chunk(ti, c, buf):
            @pl.loop(0, SC_CHUNK // SC_ROW_GROUP)
            def _(rg):
                r0 = rg * SC_ROW_GROUP
                lo_acc = [jnp.zeros((ln,), F32) for _ in range(SC_ROW_GROUP)]
                hi_acc = [jnp.zeros((ln,), F32) for _ in range(SC_ROW_GROUP)]
                for j in range(half // ln):
                    xl = hn_v[ti, pl.ds(j * ln, ln)]
                    xh = hn_v[ti, pl.ds(half + j * ln, ln)]
                    for i in range(SC_ROW_GROUP):
                        lo, hi = words(buf[r0 + i, pl.ds(j * ln, ln)])
                        lo_acc[i] = lo_acc[i] + lo * xl
                        hi_acc[i] = hi_acc[i] + hi * xh
                for i in range(SC_ROW_GROUP):
                    part[c * SC_CHUNK + r0 + i, :] = lo_acc[i] + hi_acc[i]

        def expert_weights(ti):
            @pl.loop(0, k // ln)
            def _(g):
                rows = g * ln + lanes
                a = jnp.zeros((ln,), F32)
                for l in range(ln):
                    a = a + plsc.load_gather(part, [rows, jnp.full((ln,), l, jnp.int32)])
                z = math.sqrt(2.0 / math.pi) * (a + 0.044715 * a * a * a)
                tanh = 1.0 - 2.0 / (jnp.exp(2.0 * z) + 1.0)
                w_v[pl.ds(g * ln, ln)] = gate_v[ti, pl.ds(g * ln, ln)] * (0.5 * a * (1.0 + tanh))

        def up_chunk(ti, c, buf):
            for jb in range(half // ln // SC_ACC_WORDS):
                cols = [(jb * SC_ACC_WORDS + jj) * ln for jj in range(SC_ACC_WORDS)]
                init = []
                for col in cols:
                    init += [x_v[ti, pl.ds(col, ln)], x_v[ti, pl.ds(half + col, ln)]]

                def body(r, accs, cols=cols):
                    ws = plsc.load_gather(w_v, [jnp.zeros((ln,), jnp.int32) + (c * SC_CHUNK + r)])
                    out = []
                    for jj, col in enumerate(cols):
                        lo, hi = words(buf[r, pl.ds(col, ln)])
                        out += [accs[2 * jj] + ws * lo, accs[2 * jj + 1] + ws * hi]
                    return tuple(out)

                accs = lax.fori_loop(0, SC_CHUNK, body, tuple(init))
                for jj, col in enumerate(cols):
                    x_v[ti, pl.ds(col, ln)] = accs[2 * jj]
                    x_v[ti, pl.ds(half + col, ln)] = accs[2 * jj + 1]

        @pl.loop(0, per_w // SC_TOKENS)
        def _(bi):
            out_base = wid * per_w + bi * SC_TOKENS
            base = first + out_base
            pltpu.sync_copy(idx_hbm.at[pl.ds(base, SC_TOKENS)], idx_v)
            pltpu.sync_copy(hn_hbm.at[pl.ds(base, SC_TOKENS)], hn_v)
            pltpu.sync_copy(gate_hbm.at[pl.ds(base, SC_TOKENS)], gate_v)
            pltpu.sync_copy(x_hbm.at[pl.ds(base, SC_TOKENS)], x_v)
            for task in range(SC_AHEAD):
                chunk_copy(0, task).start()

            @pl.loop(0, SC_TOKENS)
            def _(ti):
                for task in range(ntask):
                    chunk_copy(ti, task).wait()
                    ahead = task + SC_AHEAD
                    if ahead < ntask:
                        chunk_copy(ti, ahead).start()
                    else:
                        @pl.when(ti + 1 < SC_TOKENS)
                        def _(ahead=ahead):
                            chunk_copy(ti + 1, ahead - ntask).start()
                    buf = ring[task % SC_SLOTS]
                    if task < nch:
                        down_chunk(ti, task, buf)
                        if task == nch - 1:
                            expert_weights(ti)
                    else:
                        up_chunk(ti, task - nch, buf)

            pltpu.sync_copy(x_v, o_hbm.at[pl.ds(out_base, SC_TOKENS)])

    return kern(idx, hn, gate, x, u16, v16)


def _peer_layer(x2, g, w_q, keys, u, v, sc_share):
    hn, gate_t, idx_rows, gate_rows = _peer_front(x2, g, w_q, keys)
    n, k = idx_rows.shape
    t = ROW_BLOCK
    nb = n // t
    nb_sc = (nb * sc_share) // SC_SHARE_DEN
    n_tc = (nb - nb_sc) * t
    out_tc = _peer_experts(x2, hn, idx_rows.reshape(n * k), gate_t, _pack_experts(u, v), nb - nb_sc)
    if nb_sc == 0:
        return out_tc
    out_sc = _sc_peer_experts(idx_rows, hn, gate_rows, x2, _pack_halves(u), _pack_halves(v), n_tc, n - n_tc)
    return jnp.concatenate([out_tc, out_sc], axis=0)


def _ple_kernel(x_ref, p_ref, g_ref, wg_ref, wp_ref, fg_ref, o_ref, *, final):
    x = x_ref[...]
    gate = jax.nn.sigmoid(_dot(_rms(x, g_ref[...]), wg_ref[...]))
    y = x + gate * _dot(p_ref[...], wp_ref[...])
    o_ref[...] = _rms(y, fg_ref[...]) if final else y


def _ple_layer(x2, p2, g, w_g, w_p, final_g, final):
    n, d = x2.shape
    t = ROW_BLOCK
    row = lambda width: pl.BlockSpec((t, width), lambda i: (i, 0))
    return pl.pallas_call(
        functools.partial(_ple_kernel, final=final),
        grid=(n // t,),
        in_specs=[row(d), row(p2.shape[1]), _full((1, d)), _full(w_g.shape), _full(w_p.shape), _full((1, d))],
        out_specs=row(d),
        out_shape=jax.ShapeDtypeStruct((n, d), F32),
        compiler_params=_params(("parallel",)),
        name="ple",
    )(x2, p2, g.reshape(1, d), w_g.astype(BF16), w_p.astype(BF16), final_g.reshape(1, d))


def kernel(x, p, positions, mix_norm, ffn_norm, pool_w, pool_scale, kv_in_norm, w_dkv, kv_norm, w_ukv, w_dq, q_norm, w_uq, w_o, peer_wq, peer_keys, peer_u, peer_v, ple_norm, ple_wg, ple_wp, final_norm):
    weights = (mix_norm, ffn_norm, pool_w, pool_scale, kv_in_norm, w_dkv, kv_norm, w_ukv, w_dq, q_norm, w_uq, w_o,
               peer_wq, peer_keys, peer_u, peer_v, ple_norm, ple_wg, ple_wp, final_norm)
    per = x.shape[0] // BATCH_CHAINS
    assert per * BATCH_CHAINS == x.shape[0]
    outs = [_trunk(x[c * per:(c + 1) * per], p[:, c * per:(c + 1) * per], positions[c * per:(c + 1) * per], *weights)
            for c in range(BATCH_CHAINS)]
    return jnp.concatenate(outs, axis=0)


def _trunk(x, p, positions, mix_norm, ffn_norm, pool_w, pool_scale, kv_in_norm, w_dkv, kv_norm, w_ukv, w_dq, q_norm, w_uq, w_o, peer_wq, peer_keys, peer_u, peer_v, ple_norm, ple_wg, ple_wp, final_norm):
    b, s, d = x.shape
    depth = p.shape[0]
    assert len(SC_SHARE_NUM) == depth
    n_a = pool_w.shape[0]
    x2 = x.reshape(b * s, d)
    cos, sin = _rope_tables(positions)
    k_sh = v_sh = None
    for i in range(depth):
        if i < n_a:
            x2 = _pool_layer(x2, s, mix_norm[i], pool_w[i], pool_scale[i])
        else:
            if i == n_a:
                k_sh, v_sh = _shared_kv(x2, cos, sin, kv_in_norm, w_dkv, kv_norm, w_ukv)
            j = i - n_a
            q = _q_proj(x2, cos, sin, mix_norm[i], w_dq[j], q_norm[j], w_uq[j])
            o = _flash_attention(q, k_sh, v_sh, b, s)
            x2 = _out_proj(x2, o, w_o[j])
        x2 = _peer_layer(x2, ffn_norm[i], peer_wq[i], peer_keys[i], peer_u[i], peer_v[i], SC_SHARE_NUM[i])
        x2 = _ple_layer(x2, p[i].reshape(b * s, -1), ple_norm[i], ple_wg[i], ple_wp[i], final_norm, i == depth - 1)
    return x2.reshape(b, s, d)
```

```python
import functools
import math

import jax
import jax.numpy as jnp
from jax import lax
from jax.experimental import pallas as pl
from jax.experimental.pallas import tpu as pltpu
from jax.experimental.pallas import tpu_sc as plsc

F32 = jnp.float32
BF16 = jnp.bfloat16
EPS = 1e-6

POOL_WINDOWS = (2, 4, 8, 16)
POOL_HALO = 16
MLA_HEADS = 8
QK_NOPE = 128
QK_ROPE = 64
V_HEAD = 128
KV_RANK = 256
ROPE_THETA = 10000.0
ATTN_SCALE = 1.0 / math.sqrt(QK_NOPE + QK_ROPE)
PEER_HEADS = 8
N_KEYS = 128
PEER_TOPK = 16
PEER_DHALF = 128

SUBLANES = 8
LANES = 128
ROW_BLOCK = 256
PEER_BLOCK = 512
POOL_BLOCK = 512
ATTN_BLOCK = 512
ATTN_DTYPE = jnp.bfloat16
GATHER_SUB = 8
VMEM_LIMIT = 56 * 1024 * 1024
SC_CORES = 2
SC_SUBCORES = 16
SC_LANES = 16
SC_TOKENS = 16
SC_CHUNK = 32
SC_SLOTS = 4
SC_AHEAD = 3
SC_ROW_GROUP = 4
SC_ACC_WORDS = 8
SC_SHARE_NUM, SC_SHARE_DEN = (48, 48, 56, 56), 64
BATCH_CHAINS = 8


def _params(sem):
    return pltpu.CompilerParams(dimension_semantics=sem, vmem_limit_bytes=VMEM_LIMIT)


def _rms(x, g):
    ms = jnp.mean(x * x, axis=-1, keepdims=True)
    return x * lax.rsqrt(ms + EPS) * g


def _dot(a, w):
    return jnp.dot(a.astype(w.dtype), w, preferred_element_type=F32)


def _split(a):
    hi = a.astype(BF16)
    return hi, (a - hi.astype(F32)).astype(BF16)


def _dot3(a, b, dims):
    mm = lambda x, y: lax.dot_general(x, y, (dims, ((), ())), preferred_element_type=F32)
    return mm(a[0], b[0]) + (mm(a[0], b[1]) + mm(a[1], b[0]))


def _full(shape):
    return pl.BlockSpec(shape, lambda *_: (0,) * len(shape))


def _pool_kernel(x_ref, halo_ref, g_ref, w_ref, sc_ref, o_ref, *, ts, blocks_per_seq):
    blk = pl.program_id(0) % blocks_per_seq
    x = x_ref[...]
    g = g_ref[...]
    h = _rms(x, g)
    hh = jnp.where(blk == 0, 0.0, _rms(halo_ref[...], g))
    full = jnp.concatenate([hh, h], axis=0)
    t = blk * ts + lax.broadcasted_iota(jnp.int32, (ts, 1), 0)
    gw = w_ref.shape[-1]
    outs = []
    for gi, w in enumerate(POOL_WINDOWS):
        p = full[:, gi * gw:(gi + 1) * gw]
        step = 1
        while step < w:
            p = p + pltpu.roll(p, step, axis=0)
            step *= 2
        win = p[POOL_HALO:, :]
        cnt = jnp.minimum(t + 1, w).astype(F32)
        pooled = win / cnt - h[:, gi * gw:(gi + 1) * gw]
        outs.append(_dot(pooled, w_ref[gi]))
    o_ref[...] = x + jnp.concatenate(outs, axis=-1) * sc_ref[...]


def _pool_layer(x2, seq, g, w, scale):
    n, d = x2.shape
    ts = min(POOL_BLOCK, seq)
    assert seq % ts == 0 and ts % POOL_HALO == 0 and all(wd & (wd - 1) == 0 and wd <= POOL_HALO for wd in POOL_WINDOWS)
    hb = ts // POOL_HALO
    return pl.pallas_call(
        functools.partial(_pool_kernel, ts=ts, blocks_per_seq=seq // ts),
        grid=(n // ts,),
        in_specs=[
            pl.BlockSpec((ts, d), lambda i: (i, 0)),
            pl.BlockSpec((POOL_HALO, d), lambda i: (jnp.maximum(i * hb - 1, 0), 0)),
            _full((1, d)),
            _full(w.shape),
            _full((1, d)),
        ],
        out_specs=pl.BlockSpec((ts, d), lambda i: (i, 0)),
        out_shape=jax.ShapeDtypeStruct((n, d), F32),
        compiler_params=_params(("parallel",)),
        name="pool_mixer",
    )(x2, x2, g.reshape(1, d), w.astype(BF16), scale.reshape(1, d))


def _rope_kernel(pos_ref, inv_ref, cos_ref, sin_ref):
    ang = inv_ref[...] * pos_ref[0].astype(F32)
    cos_ref[0] = jnp.cos(ang)
    sin_ref[0] = jnp.sin(ang)


def _rope_tables(positions):
    b, s = positions.shape
    half = QK_ROPE // 2
    inv = ROPE_THETA ** (-jnp.arange(0, QK_ROPE, 2, dtype=F32) / QK_ROPE)
    cos_t, sin_t = pl.pallas_call(
        _rope_kernel,
        grid=(b,),
        in_specs=[pl.BlockSpec((1, 1, s), lambda i: (i, 0, 0)), _full((half, 1))],
        out_specs=[pl.BlockSpec((1, half, s), lambda i: (i, 0, 0))] * 2,
        out_shape=[jax.ShapeDtypeStruct((b, half, s), F32)] * 2,
        compiler_params=_params(("parallel",)),
        name="rope_tables",
    )(positions.reshape(b, 1, s), inv.reshape(half, 1))
    to_rows = lambda a: a.transpose(0, 2, 1).reshape(b * s, half)
    return to_rows(cos_t), to_rows(sin_t)


def _kv_kernel(x_ref, cos_ref, sin_ref, gin_ref, wdkv_ref, gkv_ref, wukv_ref, k_ref, v_ref):
    h = _rms(x_ref[...], gin_ref[...])
    ckv = _dot(h, wdkv_ref[...])
    c = _rms(ckv[:, :KV_RANK], gkv_ref[...])
    half = QK_ROPE // 2
    x1 = ckv[:, KV_RANK:KV_RANK + half]
    x2 = ckv[:, KV_RANK + half:]
    cos = cos_ref[...]
    sin = sin_ref[...]
    k1 = x1 * cos - x2 * sin
    k2 = x2 * cos + x1 * sin
    kv = _dot(c, wukv_ref[...])
    for hd in range(MLA_HEADS):
        k_ref[hd] = jnp.concatenate([kv[:, hd * QK_NOPE:(hd + 1) * QK_NOPE], k1, k2], axis=-1).astype(k_ref.dtype)
        off = MLA_HEADS * QK_NOPE + hd * V_HEAD
        ones_col = (lax.broadcasted_iota(jnp.int32, (kv.shape[0], V_HEAD), 1) == 0).astype(F32)
        v_ref[hd] = jnp.concatenate([kv[:, off:off + V_HEAD], ones_col], axis=-1).astype(v_ref.dtype)


def _split_heads(w, bounds):
    w3 = w.reshape(w.shape[0], MLA_HEADS, -1)
    return jnp.concatenate([w3[:, :, lo:hi].reshape(w.shape[0], -1) for lo, hi in bounds], axis=1)


def _shared_kv(x2, cos, sin, kv_in_norm, w_dkv, kv_norm, w_ukv):
    n, d = x2.shape
    t = ROW_BLOCK
    half = QK_ROPE // 2
    w_ukv_p = _split_heads(w_ukv, [(0, QK_NOPE), (QK_NOPE, QK_NOPE + V_HEAD)])
    row = lambda width: pl.BlockSpec((t, width), lambda i: (i, 0))
    return pl.pallas_call(
        _kv_kernel,
        grid=(n // t,),
        in_specs=[row(d), row(half), row(half), _full((1, d)), _full(w_dkv.shape), _full((1, KV_RANK)),
                  _full(w_ukv_p.shape)],
        out_specs=[pl.BlockSpec((MLA_HEADS, t, QK_NOPE + QK_ROPE), lambda i: (0, i, 0)),
                   pl.BlockSpec((MLA_HEADS, t, 2 * V_HEAD), lambda i: (0, i, 0))],
        out_shape=[jax.ShapeDtypeStruct((MLA_HEADS, n, QK_NOPE + QK_ROPE), ATTN_DTYPE),
                   jax.ShapeDtypeStruct((MLA_HEADS, n, 2 * V_HEAD), ATTN_DTYPE)],
        compiler_params=_params(("parallel",)),
        name="shared_kv",
    )(x2, cos, sin, kv_in_norm.reshape(1, d), w_dkv.astype(BF16), kv_norm.reshape(1, KV_RANK), w_ukv_p.astype(BF16))


def _q_kernel(x_ref, cos_ref, sin_ref, g_ref, wdq_ref, gq_ref, wuq_ref, q_ref):
    h = _rms(x_ref[...], g_ref[...])
    cq = _rms(_dot(h, wdq_ref[...]), gq_ref[...])
    q = _dot(cq, wuq_ref[...]) * ATTN_SCALE
    half = QK_ROPE // 2
    nope = MLA_HEADS * QK_NOPE
    cos = jnp.concatenate([cos_ref[...]] * MLA_HEADS, axis=-1)
    sin = jnp.concatenate([sin_ref[...]] * MLA_HEADS, axis=-1)
    p1 = q[:, nope:nope + MLA_HEADS * half]
    p2 = q[:, nope + MLA_HEADS * half:]
    r1 = p1 * cos - p2 * sin
    r2 = p2 * cos + p1 * sin
    for hd in range(MLA_HEADS):
        q_ref[hd] = jnp.concatenate([q[:, hd * QK_NOPE:(hd + 1) * QK_NOPE], r1[:, hd * half:(hd + 1) * half],
                                     r2[:, hd * half:(hd + 1) * half]], axis=-1).astype(q_ref.dtype)


def _q_proj(x2, cos, sin, g, w_dq, q_norm, w_uq):
    n, d = x2.shape
    t = ROW_BLOCK
    half = QK_ROPE // 2
    rank = w_dq.shape[1]
    w_uq_p = _split_heads(w_uq, [(0, QK_NOPE), (QK_NOPE, QK_NOPE + half), (QK_NOPE + half, QK_NOPE + QK_ROPE)])
    row = lambda width: pl.BlockSpec((t, width), lambda i: (i, 0))
    return pl.pallas_call(
        _q_kernel,
        grid=(n // t,),
        in_specs=[row(d), row(half), row(half), _full((1, d)), _full(w_dq.shape), _full((1, rank)),
                  _full(w_uq_p.shape)],
        out_specs=pl.BlockSpec((MLA_HEADS, t, QK_NOPE + QK_ROPE), lambda i: (0, i, 0)),
        out_shape=jax.ShapeDtypeStruct((MLA_HEADS, n, QK_NOPE + QK_ROPE), ATTN_DTYPE),
        compiler_params=_params(("parallel",)),
        name="mla_q",
    )(x2, cos, sin, g.reshape(1, d), w_dq.astype(BF16), q_norm.reshape(1, rank), w_uq_p.astype(BF16))


def _flash_kernel(q_ref, k_ref, v_ref, o_ref, m_sc, l_sc, acc_sc, *, tb):
    qi = pl.program_id(2)
    ki = pl.program_id(3)

    @pl.when(ki == 0)
    def _():
        m_sc[...] = jnp.full_like(m_sc, -jnp.inf)
        l_sc[...] = jnp.zeros_like(l_sc)
        acc_sc[...] = jnp.zeros_like(acc_sc)

    def update(masked):
        s = lax.dot_general(q_ref[0], k_ref[0], (((1,), (1,)), ((), ())), preferred_element_type=F32)
        if masked:
            row = lax.broadcasted_iota(jnp.int32, (tb, tb), 0)
            col = lax.broadcasted_iota(jnp.int32, (tb, tb), 1)
            s = jnp.where(col <= row, s, -jnp.inf)
        m_prev = m_sc[...]
        m_new = jnp.maximum(m_prev, jnp.max(s, axis=-1, keepdims=True))
        p = jnp.exp(s - m_new)
        alpha = jnp.exp(m_prev - m_new)
        acc_sc[...] = alpha * acc_sc[...] + jnp.dot(p.astype(v_ref.dtype), v_ref[0], preferred_element_type=F32)
        m_sc[...] = m_new

    pl.when(ki < qi)(functools.partial(update, False))
    pl.when(ki == qi)(functools.partial(update, True))

    @pl.when(ki == qi)
    def _():
        acc = acc_sc[...]
        o_ref[...] = (acc[:, :V_HEAD] / acc[:, V_HEAD:V_HEAD + 1]).astype(o_ref.dtype)


def _flash_attention(q, k, v, batch, seq):
    tb = min(ATTN_BLOCK, seq)
    nb = seq // tb
    n = batch * seq
    dk = q.shape[-1]
    return pl.pallas_call(
        functools.partial(_flash_kernel, tb=tb),
        grid=(batch, MLA_HEADS, nb, nb),
        in_specs=[
            pl.BlockSpec((1, tb, dk), lambda b, h, qi, ki: (h, b * nb + qi, 0)),
            pl.BlockSpec((1, tb, dk), lambda b, h, qi, ki: (h, b * nb + jnp.minimum(ki, qi), 0)),
            pl.BlockSpec((1, tb, 2 * V_HEAD), lambda b, h, qi, ki: (h, b * nb + jnp.minimum(ki, qi), 0)),
        ],
        out_specs=pl.BlockSpec((tb, V_HEAD), lambda b, h, qi, ki: (b * nb + qi, h)),
        out_shape=jax.ShapeDtypeStruct((n, MLA_HEADS * V_HEAD), ATTN_DTYPE),
        scratch_shapes=[pltpu.VMEM((tb, 1), F32), pltpu.VMEM((tb, 1), F32), pltpu.VMEM((tb, 2 * V_HEAD), F32)],
        compiler_params=_params(("parallel", "parallel", "parallel", "arbitrary")),
        name="mla_flash",
    )(q, k, v)


def _oproj_kernel(x_ref, o_ref, w_ref, y_ref):
    y_ref[...] = x_ref[...] + _dot(o_ref[...], w_ref[...])


def _out_proj(x2, o, w_o):
    n, d = x2.shape
    t = ROW_BLOCK
    row = lambda width: pl.BlockSpec((t, width), lambda i: (i, 0))
    return pl.pallas_call(
        _oproj_kernel,
        grid=(n // t,),
        in_specs=[row(d), row(o.shape[1]), _full(w_o.shape)],
        out_specs=row(d),
        out_shape=jax.ShapeDtypeStruct((n, d), F32),
        compiler_params=_params(("parallel",)),
        name="mla_out",
    )(x2, o, w_o.astype(BF16))


def _top_rows(problems, k):
    def pick(s, payload, r, vals, idxs):
        rows, t = s.shape
        iota = lax.broadcasted_iota(jnp.int32, (rows, t), 0)
        out_row = lax.broadcasted_iota(jnp.int32, (k, t), 0)
        m = jnp.max(s, axis=0, keepdims=True)
        pos = jnp.min(jnp.where(s == m, iota, rows), axis=0, keepdims=True)
        hit = iota == pos
        got = pos if payload is None else jnp.sum(jnp.where(hit, payload, 0), axis=0, keepdims=True)
        return (jnp.where(hit, -jnp.inf, s), jnp.where(out_row == r, m, vals), jnp.where(out_row == r, got, idxs))

    def body(r, carry):
        return tuple(pick(s, payload, r, vals, idxs) for (s, vals, idxs), (_, payload) in zip(carry, problems))

    t = problems[0][0].shape[1]
    init = tuple((s, jnp.zeros((k, t), F32), jnp.zeros((k, t), jnp.int32)) for s, _ in problems)
    return [(vals, idxs) for _, vals, idxs in lax.fori_loop(0, k, body, init)]


def _pair_candidates(s1, i1, s2, i2):
    neg = jnp.float32(-jnp.inf)
    sub = lax.broadcasted_iota(jnp.int32, (SUBLANES, s1.shape[1]), 0)
    cs, ci = [s1[0:1] + s2], [i1[0:1] * N_KEYS + i2]
    for a in range(1, PEER_TOPK // 2):
        keep = sub < PEER_TOPK // (a + 1)
        cs.append(jnp.where(keep, s1[a:a + 1] + s2[:SUBLANES], neg))
        ci.append(i1[a:a + 1] * N_KEYS + i2[:SUBLANES])
    cs.append(s1[PEER_TOPK // 2:] + s2[0:1])
    ci.append(i1[PEER_TOPK // 2:] * N_KEYS + i2[0:1])
    return jnp.concatenate(cs, axis=0), jnp.concatenate(ci, axis=0)


def _peer_front_kernel(x_ref, g_ref, wq_hi_ref, wq_lo_ref, keys_hi_ref, keys_lo_ref, hn_ref, gate_ref, idx_rows_ref,
                       gate_rows_ref, idx_sc, gate_sc, *, lane_chunk):
    head = pl.program_id(1)
    head_rows = pl.ds(pl.multiple_of(head * PEER_TOPK, PEER_TOPK), PEER_TOPK)
    hn = _rms(x_ref[...], g_ref[...])

    @pl.when(head == 0)
    def _():
        hn_ref[...] = hn

    q = _dot3(_split(hn), (wq_hi_ref[...], wq_lo_ref[...]), ((1,), (0,)))
    cands = []
    for c in range(hn.shape[0] // lane_chunk):
        tok = slice(c * lane_chunk, (c + 1) * lane_chunk)
        scores = [_dot3((keys_hi_ref[0, half], keys_lo_ref[0, half]),
                        _split(q[tok, half * PEER_DHALF:(half + 1) * PEER_DHALF]), ((1,), (1,))) for half in range(2)]
        (s1, i1), (s2, i2) = _top_rows([(sc, None) for sc in scores], PEER_TOPK)
        cands.append(_pair_candidates(s1, i1, s2, i2))
    best = []
    for c in range(0, len(cands), 2):
        best += _top_rows(cands[c:c + 2], PEER_TOPK)
    for c, (best_s, best_i) in enumerate(best):
        tok = slice(c * lane_chunk, (c + 1) * lane_chunk)
        e = jnp.exp(best_s - best_s[0:1, :])
        gate = e / jnp.sum(e, axis=0, keepdims=True)
        gate_ref[0, :, tok] = gate
        gate_sc[head_rows, tok] = gate
        idx_sc[head_rows, tok] = best_i

    @pl.when(head == pl.num_programs(1) - 1)
    def _():
        idx_rows_ref[...] = idx_sc[...].T
        gate_rows_ref[...] = gate_sc[...].T


def _peer_front(x2, g, w_q, keys):
    n, d = x2.shape
    t = PEER_BLOCK
    nb = n // t
    dq = 2 * PEER_DHALF
    k = PEER_HEADS * PEER_TOPK
    return pl.pallas_call(
        functools.partial(_peer_front_kernel, lane_chunk=128),
        grid=(nb, PEER_HEADS),
        in_specs=[
            pl.BlockSpec((t, d), lambda i, h: (i, 0)),
            _full((1, d)),
            pl.BlockSpec((d, dq), lambda i, h: (0, h)),
            pl.BlockSpec((d, dq), lambda i, h: (0, h)),
            pl.BlockSpec((1, 2, N_KEYS, PEER_DHALF), lambda i, h: (h, 0, 0, 0)),
            pl.BlockSpec((1, 2, N_KEYS, PEER_DHALF), lambda i, h: (h, 0, 0, 0)),
        ],
        out_specs=[
            pl.BlockSpec((t, d), lambda i, h: (i, 0)),
            pl.BlockSpec((1, PEER_TOPK, t), lambda i, h: (i, h, 0)),
            pl.BlockSpec((t, k), lambda i, h: (i, 0)),
            pl.BlockSpec((t, k), lambda i, h: (i, 0)),
        ],
        out_shape=[
            jax.ShapeDtypeStruct((n, d), F32),
            jax.ShapeDtypeStruct((nb, k, t), F32),
            jax.ShapeDtypeStruct((n, k), jnp.int32),
            jax.ShapeDtypeStruct((n, k), F32),
        ],
        scratch_shapes=[pltpu.VMEM((k, t), jnp.int32), pltpu.VMEM((k, t), F32)],
        compiler_params=_params(("parallel", "arbitrary")),
        name="peer_front",
    )(x2, g.reshape(1, d), *_split(w_q), *_split(keys))


def _row_sums(p):
    k = p.shape[0]
    p = p.reshape(k // SUBLANES, SUBLANES, SUBLANES, LANES)
    s = lax.broadcasted_iota(jnp.int32, (1, 1, SUBLANES, LANES), 2)
    a, b = p[:, :4], p[:, 4:]
    m = (s & 4) == 0
    p = jnp.where(m, a, b) + pltpu.roll(jnp.where(m, b, a), 4, axis=2)
    for dist in (2, 1):
        h = p.shape[1] // 2
        a, b = p[:, :h], p[:, h:]
        m = (s & dist) == 0
        p = jnp.where(m, a + pltpu.roll(a, SUBLANES - dist, axis=2), b + pltpu.roll(b, dist, axis=2))
    return p.reshape(k, LANES)


def _swap_rows_and_chunks(parts):
    parts = list(parts)
    q = lax.broadcasted_iota(jnp.int32, parts[0].shape, 1)
    for d in (4, 2, 1):
        low = (q & d) == 0
        for p0 in range(SUBLANES):
            if p0 & d:
                continue
            a, b = parts[p0], parts[p0 + d]
            parts[p0] = jnp.where(low, a, pltpu.roll(b, d, axis=1))
            parts[p0 + d] = jnp.where(low, pltpu.roll(a, SUBLANES - d, axis=1), b)
    return parts


def _peer_experts_kernel(idx_ref, idx_next_ref, hn_ref, gate_ref, x_ref, uv_hbm, o_ref, buf_a, buf_b, wbuf, hbuf, obuf, sem,
                         *, sub):
    step = pl.program_id(0)
    last_step = pl.num_programs(0) - 1
    t = hn_ref.shape[0]
    k = gate_ref.shape[1]
    n_sub = t // sub
    bufs = (buf_a, buf_b)
    lane = lax.broadcasted_iota(jnp.int32, (1, t), 1)
    hi_mask = jnp.int32(-65536)

    def issue_token(idx, base, tk, slot):
        for r in range(k):
            e = idx[base + tk * k + r]
            pltpu.make_async_copy(uv_hbm.at[e], bufs[slot].at[tk * k + r], sem.at[slot]).start(priority=r % 2)

    def wait(slot):
        pltpu.make_async_copy(uv_hbm.at[pl.ds(0, sub * k)], bufs[slot], sem.at[slot]).wait()

    def compute_token(sb, tk, slot):
        buf = bufs[slot]
        tok = sb * sub + tk
        u = lax.bitcast_convert_type(buf[pl.ds(tk * k, k)] << 16, F32)
        a = jnp.sum(_row_sums(u * hbuf[tk][None]), axis=-1, keepdims=True)
        gcol = jnp.sum(jnp.where(lane == tok, gate_ref[0], 0.0), axis=-1, keepdims=True)
        wbuf[tk] = jnp.broadcast_to(gcol * jax.nn.gelu(a), (k, LANES))
        acc = [None] * 4
        for r in range(k):
            v = lax.bitcast_convert_type(buf[tk * k + r] & hi_mask, F32)
            term = jnp.broadcast_to(wbuf[tk, r:r + 1, :], (SUBLANES, LANES)) * v
            acc[r % 4] = term if acc[r % 4] is None else acc[r % 4] + term
        obuf[tk] = (acc[0] + acc[1]) + (acc[2] + acc[3])

    def run_sub(sb, slot, next_idx, next_base):
        rows = pl.ds(pl.multiple_of(sb * sub, SUBLANES), sub)
        chunk = lambda j: slice(j * LANES, (j + 1) * LANES)
        hn = hn_ref[rows, :]
        for r, tile in enumerate(_swap_rows_and_chunks([hn[:, chunk(j)][None] for j in range(SUBLANES)])):
            hbuf[r] = tile[0]
        wait(slot)

        def body(tk, carry):
            if next_idx is not None:
                issue_token(next_idx, next_base, tk, 1 - slot)
            compute_token(sb, tk, slot)
            return carry

        lax.fori_loop(0, sub, body, 0)
        for j, part in enumerate(_swap_rows_and_chunks([obuf[r][None] for r in range(sub)])):
            o_ref[rows, chunk(j)] = x_ref[rows, chunk(j)] + part[0]

    @pl.when(step == 0)
    def _():
        lax.fori_loop(0, sub, lambda tk, c: (issue_token(idx_ref, 0, tk, 0), c)[1], 0)

    def pair(sp, carry):
        run_sub(2 * sp, 0, idx_ref, (2 * sp + 1) * sub * k)
        run_sub(2 * sp + 1, 1, idx_ref, (2 * sp + 2) * sub * k)
        return carry

    lax.fori_loop(0, n_sub // 2 - 1, pair, 0)
    run_sub(n_sub - 2, 0, idx_ref, (n_sub - 1) * sub * k)

    @pl.when(step < last_step)
    def _():
        run_sub(n_sub - 1, 1, idx_next_ref, 0)

    @pl.when(step == last_step)
    def _():
        run_sub(n_sub - 1, 1, None, 0)


def _peer_experts(x2, hn, idx_flat, gate_t, uv, nb):
    d = x2.shape[1]
    k = gate_t.shape[1]
    t = ROW_BLOCK
    per_front = gate_t.shape[2] // t
    n = nb * t
    sub = GATHER_SUB
    assert t % (2 * sub) == 0 and sub == SUBLANES and d == SUBLANES * LANES
    row = pl.BlockSpec((t, d), lambda i: (i, 0))
    return pl.pallas_call(
        functools.partial(_peer_experts_kernel, sub=sub),
        grid=(nb,),
        in_specs=[
            pl.BlockSpec((t * k,), lambda i: (i,), memory_space=pltpu.SMEM),
            pl.BlockSpec((t * k,), lambda i: (jnp.minimum(i + 1, nb - 1),), memory_space=pltpu.SMEM),
            row,
            pl.BlockSpec((1, k, t), lambda i: (i // per_front, 0, i % per_front)),
            row,
            pl.BlockSpec(memory_space=pl.ANY),
        ],
        out_specs=row,
        out_shape=jax.ShapeDtypeStruct((n, d), F32),
        scratch_shapes=[
            pltpu.VMEM((sub * k, SUBLANES, LANES), jnp.int32),
            pltpu.VMEM((sub * k, SUBLANES, LANES), jnp.int32),
            pltpu.VMEM((sub, k, LANES), F32),
            pltpu.VMEM((sub, SUBLANES, LANES), F32),
            pltpu.VMEM((sub, SUBLANES, LANES), F32),
            pltpu.SemaphoreType.DMA((2,)),
        ],
        compiler_params=_params(("arbitrary",)),
        name="peer_experts",
    )(idx_flat, idx_flat, hn, gate_t, x2, uv)


def _pack_kernel(u_ref, v_ref, o_ref):
    bits = lambda a: lax.bitcast_convert_type(a.astype(BF16).astype(F32), jnp.int32)
    words = (bits(v_ref[...]) & jnp.int32(-65536)) | lax.shift_right_logical(bits(u_ref[...]), 16)
    g = words.shape[0] // SUBLANES
    parts = [words[:, j * LANES:(j + 1) * LANES].reshape(g, SUBLANES, LANES) for j in range(SUBLANES)]
    for r, tile in enumerate(_swap_rows_and_chunks(parts)):
        o_ref[:, r] = tile


def _pack_experts(u, v):
    e, d = u.shape
    t = ROW_BLOCK
    assert d == SUBLANES * LANES and e % t == 0
    out = pl.pallas_call(
        _pack_kernel,
        grid=(e // t,),
        in_specs=[pl.BlockSpec((t, d), lambda i: (i, 0))] * 2,
        out_specs=pl.BlockSpec((t // SUBLANES, SUBLANES, SUBLANES, LANES), lambda i: (i, 0, 0, 0)),
        out_shape=jax.ShapeDtypeStruct((e // SUBLANES, SUBLANES, SUBLANES, LANES), jnp.int32),
        compiler_params=_params(("parallel",)),
        name="pack_experts",
    )(u, v)
    return out.reshape(e, SUBLANES, LANES)


def _pack_halves(t):
    half = t.shape[1] // 2
    bits = lambda a: lax.bitcast_convert_type(a.astype(BF16), jnp.uint16).astype(jnp.uint32)
    return lax.bitcast_convert_type((bits(t[:, half:]) << 16) | bits(t[:, :half]), jnp.int32)


def _sc_peer_experts(idx, hn, gate, x, u16, v16, first, n):
    k = idx.shape[1]
    d = hn.shape[1]
    half = d // 2
    ln = SC_LANES
    workers = SC_CORES * SC_SUBCORES
    nch = k // SC_CHUNK
    ntask = 2 * nch
    assert n % (workers * SC_TOKENS) == 0 and half % (ln * SC_ACC_WORDS) == 0 and SC_CHUNK % SC_ROW_GROUP == 0
    assert ntask % SC_SLOTS == 0 and SC_AHEAD < SC_SLOTS
    per_w = n // workers
    mesh = plsc.VectorSubcoreMesh(core_axis_name="c", subcore_axis_name="s")
    hi_mask = jnp.int32(-65536)

    def words(w):
        return lax.bitcast_convert_type(w << 16, F32), lax.bitcast_convert_type(w & hi_mask, F32)

    @functools.partial(
        pl.kernel, mesh=mesh, out_type=jax.ShapeDtypeStruct((n, d), F32),
        scratch_types=[
            pltpu.VMEM((SC_TOKENS, k), jnp.int32),
            pltpu.VMEM((SC_TOKENS, d), F32),
            pltpu.VMEM((SC_TOKENS, k), F32),
            pltpu.VMEM((SC_TOKENS, d), F32),
            [pltpu.VMEM((SC_CHUNK, half), jnp.int32)] * SC_SLOTS,
            pltpu.VMEM((k, ln), F32),
            pltpu.VMEM((k,), F32),
            [pltpu.SemaphoreType.DMA] * SC_SLOTS,
        ],
        compiler_params=pltpu.CompilerParams(needs_layout_passes=False, use_tc_tiling_on_sc=True),
        name="peer_experts_sc",
    )
    def kern(idx_hbm, hn_hbm, gate_hbm, x_hbm, u_hbm, v_hbm, o_hbm, idx_v, hn_v, gate_v, x_v, ring, part, w_v, sems):
        wid = lax.axis_index("s") * SC_CORES + lax.axis_index("c")
        lanes = lax.iota(jnp.int32, ln)

        def chunk_copy(ti, task):
            table = u_hbm if task < nch else v_hbm
            rows = idx_v.at[ti, pl.ds((task % nch) * SC_CHUNK, SC_CHUNK)]
            return pltpu.make_async_copy(table.at[rows], ring[task % SC_SLOTS], sems[task % SC_SLOTS])

        def down_chunk(ti, c, buf):
            @pl.loop(0, SC_CHUNK // SC_ROW_GROUP)
            def _(rg):
                r0 = rg * SC_ROW_GROUP
                lo_acc = [jnp.zeros((ln,), F32) for _ in range(SC_ROW_GROUP)]
                hi_acc = [jnp.zeros((ln,), F32) for _ in range(SC_ROW_GROUP)]
                for j in range(half // ln):
                    xl = hn_v[ti, pl.ds(j * ln, ln)]
                    xh = hn_v[ti, pl.ds(half + j * ln, ln)]
                    for i in range(SC_ROW_GROUP):
                        lo, hi = words(buf[r0 + i, pl.ds(j * ln, ln)])
                        lo_acc[i] = lo_acc[i] + lo * xl
                        hi_acc[i] = hi_acc[i] + hi * xh
                for i in range(SC_ROW_GROUP):
                    part[c * SC_CHUNK + r0 + i, :] = lo_acc[i] + hi_acc[i]

        def expert_weights(ti):
            @pl.loop(0, k // ln)
            def _(g):
                rows = g * ln + lanes
                a = jnp.zeros((ln,), F32)
                for l in range(ln):
                    a = a + plsc.load_gather(part, [rows, jnp.full((ln,), l, jnp.int32)])
                z = math.sqrt(2.0 / math.pi) * (a + 0.044715 * a * a * a)
                tanh = 1.0 - 2.0 / (jnp.exp(2.0 * z) + 1.0)
                w_v[pl.ds(g * ln, ln)] = gate_v[ti, pl.ds(g * ln, ln)] * (0.5 * a * (1.0 + tanh))

        def up_chunk(ti, c, buf):
            for jb in range(half // ln // SC_ACC_WORDS):
                cols = [(jb * SC_ACC_WORDS + jj) * ln for jj in range(SC_ACC_WORDS)]
                init = []
                for col in cols:
                    init += [x_v[ti, pl.ds(col, ln)], x_v[ti, pl.ds(half + col, ln)]]

                def body(r, accs, cols=cols):
                    ws = plsc.load_gather(w_v, [jnp.zeros((ln,), jnp.int32) + (c * SC_CHUNK + r)])
                    out = []
                    for jj, col in enumerate(cols):
                        lo, hi = words(buf[r, pl.ds(col, ln)])
                        out += [accs[2 * jj] + ws * lo, accs[2 * jj + 1] + ws * hi]
                    return tuple(out)

                accs = lax.fori_loop(0, SC_CHUNK, body, tuple(init))
                for jj, col in enumerate(cols):
                    x_v[ti, pl.ds(col, ln)] = accs[2 * jj]
                    x_v[ti, pl.ds(half + col, ln)] = accs[2 * jj + 1]

        @pl.loop(0, per_w // SC_TOKENS)
        def _(bi):
            out_base = wid * per_w + bi * SC_TOKENS
            base = first + out_base
            pltpu.sync_copy(idx_hbm.at[pl.ds(base, SC_TOKENS)], idx_v)
            pltpu.sync_copy(hn_hbm.at[pl.ds(base, SC_TOKENS)], hn_v)
            pltpu.sync_copy(gate_hbm.at[pl.ds(base, SC_TOKENS)], gate_v)
            pltpu.sync_copy(x_hbm.at[pl.ds(base, SC_TOKENS)], x_v)
            for task in range(SC_AHEAD):
                chunk_copy(0, task).start()

            @pl.loop(0, SC_TOKENS)
            def _(ti):
                for task in range(ntask):
                    chunk_copy(ti, task).wait()
                    ahead = task + SC_AHEAD
                    if ahead < ntask:
                        chunk_copy(ti, ahead).start()
                    else:
                        @pl.when(ti + 1 < SC_TOKENS)
                        def _(ahead=ahead):
                            chunk_copy(ti + 1, ahead - ntask).start()
                    buf = ring[task % SC_SLOTS]
                    if task < nch:
                        down_chunk(ti, task, buf)
                        if task == nch - 1:
                            expert_weights(ti)
                    else:
                        up_chunk(ti, task - nch, buf)

            pltpu.sync_copy(x_v, o_hbm.at[pl.ds(out_base, SC_TOKENS)])

    return kern(idx, hn, gate, x, u16, v16)


def _peer_layer(x2, g, w_q, keys, u, v, sc_share):
    hn, gate_t, idx_rows, gate_rows = _peer_front(x2, g, w_q, keys)
    n, k = idx_rows.shape
    t = ROW_BLOCK
    nb = n // t
    nb_sc = (nb * sc_share) // SC_SHARE_DEN
    n_tc = (nb - nb_sc) * t
    out_tc = _peer_experts(x2, hn, idx_rows.reshape(n * k), gate_t, _pack_experts(u, v), nb - nb_sc)
    if nb_sc == 0:
        return out_tc
    out_sc = _sc_peer_experts(idx_rows, hn, gate_rows, x2, _pack_halves(u), _pack_halves(v), n_tc, n - n_tc)
    return jnp.concatenate([out_tc, out_sc], axis=0)


def _ple_kernel(x_ref, p_ref, g_ref, wg_ref, wp_ref, fg_ref, o_ref, *, final):
    x = x_ref[...]
    gate = jax.nn.sigmoid(_dot(_rms(x, g_ref[...]), wg_ref[...]))
    y = x + gate * _dot(p_ref[...], wp_ref[...])
    o_ref[...] = _rms(y, fg_ref[...]) if final else y


def _ple_layer(x2, p2, g, w_g, w_p, final_g, final):
    n, d = x2.shape
    t = ROW_BLOCK
    row = lambda width: pl.BlockSpec((t, width), lambda i: (i, 0))
    return pl.pallas_call(
        functools.partial(_ple_kernel, final=final),
        grid=(n // t,),
        in_specs=[row(d), row(p2.shape[1]), _full((1, d)), _full(w_g.shape), _full(w_p.shape), _full((1, d))],
        out_specs=row(d),
        out_shape=jax.ShapeDtypeStruct((n, d), F32),
        compiler_params=_params(("parallel",)),
        name="ple",
    )(x2, p2, g.reshape(1, d), w_g.astype(BF16), w_p.astype(BF16), final_g.reshape(1, d))


def kernel(x, p, positions, mix_norm, ffn_norm, pool_w, pool_scale, kv_in_norm, w_dkv, kv_norm, w_ukv, w_dq, q_norm, w_uq, w_o, peer_wq, peer_keys, peer_u, peer_v, ple_norm, ple_wg, ple_wp, final_norm):
    weights = (mix_norm, ffn_norm, pool_w, pool_scale, kv_in_norm, w_dkv, kv_norm, w_ukv, w_dq, q_norm, w_uq, w_o,
               peer_wq, peer_keys, peer_u, peer_v, ple_norm, ple_wg, ple_wp, final_norm)
    per = x.shape[0] // BATCH_CHAINS
    assert per * BATCH_CHAINS == x.shape[0]
    outs = [_trunk(x[c * per:(c + 1) * per], p[:, c * per:(c + 1) * per], positions[c * per:(c + 1) * per], *weights)
            for c in range(BATCH_CHAINS)]
    return jnp.concatenate(outs, axis=0)


def _trunk(x, p, positions, mix_norm, ffn_norm, pool_w, pool_scale, kv_in_norm, w_dkv, kv_norm, w_ukv, w_dq, q_norm, w_uq, w_o, peer_wq, peer_keys, peer_u, peer_v, ple_norm, ple_wg, ple_wp, final_norm):
    b, s, d = x.shape
    depth = p.shape[0]
    assert len(SC_SHARE_NUM) == depth
    n_a = pool_w.shape[0]
    x2 = x.reshape(b * s, d)
    cos, sin = _rope_tables(positions)
    k_sh = v_sh = None
    for i in range(depth):
        if i < n_a:
            x2 = _pool_layer(x2, s, mix_norm[i], pool_w[i], pool_scale[i])
        else:
            if i == n_a:
                k_sh, v_sh = _shared_kv(x2, cos, sin, kv_in_norm, w_dkv, kv_norm, w_ukv)
            j = i - n_a
            q = _q_proj(x2, cos, sin, mix_norm[i], w_dq[j], q_norm[j], w_uq[j])
            o = _flash_attention(q, k_sh, v_sh, b, s)
            x2 = _out_proj(x2, o, w_o[j])
        x2 = _peer_layer(x2, ffn_norm[i], peer_wq[i], peer_keys[i], peer_u[i], peer_v[i], SC_SHARE_NUM[i])
        x2 = _ple_layer(x2, p[i].reshape(b * s, -1), ple_norm[i], ple_wg[i], ple_wp[i], final_norm, i == depth - 1)
    return x2.reshape(b, s, d)
```
